```python
import jax, jax.numpy as jnp
from jax import lax
import numpy as np

D_MODEL = 1024
BATCH = 8
SEQ = 8192
DEPTH = 4
DEC_BATCH = 2
DEC_SEQ = 16384
PAST_LEN = 128

N_EVEN = (DEPTH + 1) // 2
N_ODD = DEPTH // 2
CHUNK = 128
GM_WIDTH = D_MODEL // 2
GM_HEADS = 4
GM_HEAD_DIM = GM_WIDTH // GM_HEADS
MLA_HEADS = 8
QK_NOPE = D_MODEL // 16
QK_ROPE = D_MODEL // 32
V_HEAD = D_MODEL // 16
Q_LORA = 3 * D_MODEL // 8
KV_LORA = D_MODEL // 4
ROPE_THETA = 10000.0
Q_BLOCK = 128
EVEN_SPLITS = [GM_WIDTH, 2 * GM_WIDTH, 2 * GM_WIDTH + Q_LORA, 2 * GM_WIDTH + Q_LORA + KV_LORA]
EVEN_IN = 2 * GM_WIDTH + Q_LORA + KV_LORA + QK_ROPE
EVEN_OUT = GM_WIDTH + MLA_HEADS * V_HEAD
C_HEADS = 8
C_INNER = D_MODEL
C_HEAD_DIM = C_INNER // C_HEADS
ODD_SPLITS = [C_INNER, 2 * C_INNER, 3 * C_INNER, 4 * C_INNER]
ODD_IN = 4 * C_INNER + 4 * C_HEADS
N_EXPERTS = 16
N_GROUPS = 4
EXPERTS_PER_GROUP = N_EXPERTS // N_GROUPS
TOP_K = 2
D_EXPERT = D_MODEL // 2
MOE_BLOCK = 128
ALPHA = (2 * DEPTH) ** 0.25
BETA = (8 * DEPTH) ** -0.25
EPS = 1e-5
F32 = jnp.float32

kernel_name = 'hybrid_gmlp_mla_mlstm_moe_encoder'


def layer_norm(x, g, b):
    xf = x.astype(F32)
    mu = xf.mean(-1, keepdims=True)
    xc = xf - mu
    var = (xc * xc).mean(-1, keepdims=True)
    return (xc * lax.rsqrt(var + EPS) * g.astype(F32) + b.astype(F32)).astype(x.dtype)


def rms_norm(x, g):
    xf = x.astype(F32)
    return (xf * lax.rsqrt((xf * xf).mean(-1, keepdims=True) + EPS) * g.astype(F32)).astype(x.dtype)


def rope_tables(S):
    inv = jnp.power(ROPE_THETA, -jnp.arange(0, QK_ROPE, 2, dtype=F32) / QK_ROPE)
    ang = jnp.arange(S, dtype=F32)[:, None] * inv[None, :]
    return jnp.cos(ang), jnp.sin(ang)


def apply_rope(x, cos, sin):
    xf = x.astype(F32)
    half = QK_ROPE // 2
    x1, x2 = xf[..., :half], xf[..., half:]
    return jnp.concatenate([x1 * cos - x2 * sin, x1 * sin + x2 * cos], axis=-1).astype(x.dtype)


def block_attention(q_nope, q_rope, k_nope, k_rope, v):
    B, S, H, _ = q_nope.shape
    nb = S // Q_BLOCK
    scale = (QK_NOPE + QK_ROPE) ** -0.5

    def to_blocks(t):
        return t.reshape(B, nb, Q_BLOCK, *t.shape[2:]).swapaxes(0, 1)

    def attend(blk):
        qn, qr = blk
        s = (jnp.einsum('bqhd,bkhd->bhqk', qn, k_nope) + jnp.einsum('bqhr,bkr->bhqk', qr, k_rope)).astype(F32) * scale
        p = jax.nn.softmax(s, axis=-1).astype(v.dtype)
        return jnp.einsum('bhqk,bkhd->bqhd', p, v)

    o = lax.map(attend, (to_blocks(q_nope), to_blocks(q_rope)))
    return o.swapaxes(0, 1).reshape(B, S, H, V_HEAD)


def even_mixer(x, w_in, gm_g, gm_b, w_sp, b_sp, q_norm_g, w_uq, kv_norm_g, w_ukv, w_out):
    B, S, _ = x.shape
    z = x @ w_in
    gu, gv, cq, ckv, kr = jnp.split(z, EVEN_SPLITS, axis=-1)
    gu = jax.nn.gelu(gu)
    gv = layer_norm(jax.nn.gelu(gv).reshape(B, S, GM_HEADS, GM_HEAD_DIM), gm_g, gm_b)
    gv = gv.reshape(B, S // CHUNK, CHUNK, GM_HEADS, GM_HEAD_DIM)
    sv = jnp.einsum('gpq,bnqgc->bnpgc', w_sp, gv) + b_sp.T[:, :, None]
    a_out = gu * sv.reshape(B, S, GM_WIDTH)
    q = (rms_norm(cq, q_norm_g) @ w_uq).reshape(B, S, MLA_HEADS, QK_NOPE + QK_ROPE)
    kv = (rms_norm(ckv, kv_norm_g) @ w_ukv).reshape(B, S, MLA_HEADS, QK_NOPE + V_HEAD)
    cos, sin = rope_tables(S)
    q_nope = q[..., :QK_NOPE]
    q_rope = apply_rope(q[..., QK_NOPE:], cos[:, None, :], sin[:, None, :])
    k_rope = apply_rope(kr, cos, sin)
    b_out = block_attention(q_nope, q_rope, kv[..., :QK_NOPE], k_rope, kv[..., QK_NOPE:])
    return jnp.concatenate([a_out, b_out.reshape(B, S, MLA_HEADS * V_HEAD)], axis=-1) @ w_out


def mlstm_chunkwise(q, k, v, ig, fg):
    B, S, H, Dh = q.shape
    nc = S // CHUNK

    def blocks(t):
        t = t.astype(F32).reshape(B, nc, CHUNK, H, *t.shape[3:])
        return jnp.moveaxis(t, 3, 2).swapaxes(0, 1)

    tri = jnp.tril(jnp.ones((CHUNK, CHUNK), bool))

    def step(carry, inp):
        C, n, m = carry
        qc, kc, vc, ic, lf = inp
        b = jnp.cumsum(lf, axis=-1)
        dmat = jnp.where(tri, b[..., :, None] - b[..., None, :] + ic[..., None, :], -jnp.inf)
        inter = b + m[..., None]
        m_t = jnp.maximum(inter, dmat.max(-1))
        w_inter = jnp.exp(inter - m_t)
        att = jnp.exp(dmat - m_t[..., None]) * jnp.einsum('bhtd,bhsd->bhts', qc, kc)
        num = w_inter[..., None] * jnp.einsum('bhtd,bhde->bhte', qc, C) + jnp.einsum('bhts,bhse->bhte', att, vc)
        den = w_inter * jnp.einsum('bhtd,bhd->bht', qc, n) + att.sum(-1)
        h = num / jnp.maximum(jnp.abs(den), jnp.exp(-m_t))[..., None]
        b_last = b[..., -1]
        g = b_last[..., None] - b + ic
        m_new = jnp.maximum(b_last + m, g.max(-1))
        decay = jnp.exp(b_last + m - m_new)
        kw = jnp.exp(g - m_new[..., None])[..., None] * kc
        C = decay[..., None, None] * C + jnp.einsum('bhsd,bhse->bhde', kw, vc)
        n = decay[..., None] * n + kw.sum(2)
        return (C, n, m_new), h

    init = (jnp.zeros((B, H, Dh, Dh), F32), jnp.zeros((B, H, Dh), F32), jnp.zeros((B, H), F32))
    _, hs = lax.scan(step, init, (blocks(q), blocks(k), blocks(v), blocks(ig), blocks(jax.nn.log_sigmoid(fg.astype(F32)))))
    return jnp.moveaxis(hs.swapaxes(0, 1), 2, 3).reshape(B, S, H, Dh)


def odd_mixer(x, w_in, b_gates, norm_g, norm_b, w_out):
    B, S, _ = x.shape
    z = x @ w_in
    q, k, v, o, g = jnp.split(z, ODD_SPLITS, axis=-1)
    hs = (B, S, C_HEADS, C_HEAD_DIM)
    q = q.reshape(hs)
    k = k.reshape(hs) * (C_HEAD_DIM ** -0.5)
    v = v.reshape(hs)
    g = (g + b_gates).astype(F32).reshape(B, S, 4, C_HEADS)
    h_fwd = mlstm_chunkwise(q, k, v, g[:, :, 0], g[:, :, 1])
    flip = lambda t: jnp.flip(t, axis=1)
    h_bwd = flip(mlstm_chunkwise(flip(q), flip(k), flip(v), flip(g[:, :, 2]), flip(g[:, :, 3])))
    h = layer_norm((h_fwd + h_bwd).astype(x.dtype), norm_g, norm_b).reshape(B, S, C_INNER)
    return (jax.nn.sigmoid(o) * h) @ w_out


def moe(x, w_router, router_bias, w_gate, w_up, w_down):
    B, S, Dm = x.shape
    xt = x.reshape(-1, Dm)
    T = xt.shape[0]
    scores = jax.nn.sigmoid((xt @ w_router).astype(F32))
    biased = scores + router_bias.astype(F32)
    grp_score = lax.top_k(biased.reshape(T, N_GROUPS, EXPERTS_PER_GROUP), 2)[0].sum(-1)
    g_sel = jnp.argmax(grp_score, axis=-1)
    in_group = (jnp.arange(N_EXPERTS) // EXPERTS_PER_GROUP)[None, :] == g_sel[:, None]
    _, idx = lax.top_k(jnp.where(in_group, biased, -jnp.inf), TOP_K)
    gate = jnp.take_along_axis(scores, idx, axis=-1)
    gate = gate / gate.sum(-1, keepdims=True)
    A = T * TOP_K
    flat_e = idx.reshape(-1)
    flat_tok = jnp.repeat(jnp.arange(T, dtype=jnp.int32), TOP_K)
    flat_w = gate.reshape(-1)
    order = jnp.argsort(flat_e)
    se = flat_e[order]
    counts = jnp.bincount(flat_e, length=N_EXPERTS)
    start = jnp.cumsum(counts) - counts
    padded = (counts + MOE_BLOCK - 1) // MOE_BLOCK * MOE_BLOCK
    pad_end = jnp.cumsum(padded)
    pad_start = pad_end - padded
    dest = pad_start[se] + jnp.arange(A) - start[se]
    nb = -(-A // MOE_BLOCK) + N_EXPERTS
    P = nb * MOE_BLOCK
    slot_tok = jnp.full((P,), T, jnp.int32).at[dest].set(flat_tok[order])
    slot_w = jnp.zeros((P,), F32).at[dest].set(flat_w[order])
    block_e = jnp.minimum(jnp.searchsorted(pad_end, jnp.arange(nb) * MOE_BLOCK, side='right'), N_EXPERTS - 1)
    x_pad = jnp.concatenate([xt, jnp.zeros((1, Dm), xt.dtype)], axis=0)

    def expert_block(args):
        e, tok, w = args
        xb = x_pad[tok]
        h = jax.nn.silu(xb @ w_gate[e]) * (xb @ w_up[e])
        return (h @ w_down[e]) * w[:, None].astype(xt.dtype)

    yb = lax.map(expert_block, (block_e, slot_tok.reshape(nb, MOE_BLOCK), slot_w.reshape(nb, MOE_BLOCK)))
    y = jnp.zeros((T + 1, Dm), xt.dtype).at[slot_tok].add(yb.reshape(P, Dm))[:T]
    return y.reshape(B, S, Dm)


def setup_inputs(seed: int = 0) -> dict:
    key = jax.random.key(seed)
    ks = jax.random.split(key, 32)

    def nrm(k, shape, fan_in, scale=1.0):
        return jax.random.normal(k, shape, F32) * (scale * fan_in ** -0.5)

    def gain(k, shape):
        return 1.0 + 0.02 * jax.random.normal(k, shape, F32)

    def bias(k, shape):
        return 0.02 * jax.random.normal(k, shape, F32)

    i_fwd = 0.1 * jax.random.normal(ks[14], (N_ODD, C_HEADS), F32)
    i_bwd = 0.1 * jax.random.normal(ks[15], (N_ODD, C_HEADS), F32)
    f_base = jnp.linspace(3.0, 6.0, C_HEADS, dtype=F32)[None, :]
    f_fwd = f_base + 0.1 * jax.random.normal(ks[16], (N_ODD, C_HEADS), F32)
    f_bwd = f_base + 0.1 * jax.random.normal(ks[17], (N_ODD, C_HEADS), F32)
    return {
        'x_prompt': jax.random.normal(ks[0], (BATCH, SEQ, D_MODEL), F32),
        'x_sample': jax.random.normal(ks[1], (DEC_BATCH, DEC_SEQ, D_MODEL), F32),
        'even_w_in': nrm(ks[2], (N_EVEN, D_MODEL, EVEN_IN), D_MODEL),
        'gm_norm_g': gain(ks[3], (N_EVEN, GM_HEADS, GM_HEAD_DIM)),
        'gm_norm_b': bias(ks[4], (N_EVEN, GM_HEADS, GM_HEAD_DIM)),
        'gm_w_spatial': nrm(ks[5], (N_EVEN, GM_HEADS, CHUNK, CHUNK), CHUNK),
        'gm_b_spatial': gain(ks[6], (N_EVEN, GM_HEADS, CHUNK)),
        'mla_q_norm_g': gain(ks[7], (N_EVEN, Q_LORA)),
        'mla_w_uq': nrm(ks[8], (N_EVEN, Q_LORA, MLA_HEADS * (QK_NOPE + QK_ROPE)), Q_LORA),
        'mla_kv_norm_g': gain(ks[9], (N_EVEN, KV_LORA)),
        'mla_w_ukv': nrm(ks[10], (N_EVEN, KV_LORA, MLA_HEADS * (QK_NOPE + V_HEAD)), KV_LORA),
        'even_w_out': nrm(ks[11], (N_EVEN, EVEN_OUT, D_MODEL), EVEN_OUT, BETA),
        'mlstm_w_in': nrm(ks[12], (N_ODD, D_MODEL, ODD_IN), D_MODEL),
        'mlstm_b_gates': jnp.concatenate([i_fwd, f_fwd, i_bwd, f_bwd], axis=-1),
        'mlstm_norm_g': gain(ks[18], (N_ODD, C_HEADS, C_HEAD_DIM)),
        'mlstm_norm_b': bias(ks[19], (N_ODD, C_HEADS, C_HEAD_DIM)),
        'mlstm_w_out': nrm(ks[20], (N_ODD, C_INNER, D_MODEL), C_INNER, BETA),
        'router_w': nrm(ks[21], (D_MODEL, N_EXPERTS), D_MODEL),
        'router_bias': 0.01 * jax.random.normal(ks[22], (N_EXPERTS,), F32),
        'moe_w_gate': nrm(ks[23], (DEPTH, N_EXPERTS, D_MODEL, D_EXPERT), D_MODEL),
        'moe_w_up': nrm(ks[24], (DEPTH, N_EXPERTS, D_MODEL, D_EXPERT), D_MODEL),
        'moe_w_down': nrm(ks[25], (DEPTH, N_EXPERTS, D_EXPERT, D_MODEL), D_EXPERT, BETA),
        'ln_mix_g': gain(ks[26], (DEPTH, D_MODEL)),
        'ln_mix_b': bias(ks[27], (DEPTH, D_MODEL)),
        'ln_ffn_g': gain(ks[28], (DEPTH, D_MODEL)),
        'ln_ffn_b': bias(ks[29], (DEPTH, D_MODEL)),
    }


def reference(x_prompt, x_sample, even_w_in, gm_norm_g, gm_norm_b, gm_w_spatial, gm_b_spatial, mla_q_norm_g, mla_w_uq, mla_kv_norm_g, mla_w_ukv, even_w_out, mlstm_w_in, mlstm_b_gates, mlstm_norm_g, mlstm_norm_b, mlstm_w_out, router_w, router_bias, moe_w_gate, moe_w_up, moe_w_down, ln_mix_g, ln_mix_b, ln_ffn_g, ln_ffn_b):
    def run(x):
        for l in range(DEPTH):
            j = l // 2
            if l % 2 == 0:
                mix = even_mixer(x, even_w_in[j], gm_norm_g[j], gm_norm_b[j], gm_w_spatial[j], gm_b_spatial[j],
                                 mla_q_norm_g[j], mla_w_uq[j], mla_kv_norm_g[j], mla_w_ukv[j], even_w_out[j])
            else:
                mix = odd_mixer(x, mlstm_w_in[j], mlstm_b_gates[j], mlstm_norm_g[j], mlstm_norm_b[j], mlstm_w_out[j])
            x = layer_norm(ALPHA * x + mix, ln_mix_g[l], ln_mix_b[l])
            ffn = moe(x, router_w, router_bias, moe_w_gate[l], moe_w_up[l], moe_w_down[l])
            x = layer_norm(ALPHA * x + ffn, ln_ffn_g[l], ln_ffn_b[l])
        return x

    y_prompt = run(x_prompt)
    y_sample = run(x_sample)
    return (y_prompt, y_sample)
```

```python
import functools

import jax
import jax.numpy as jnp
from jax import lax
from jax.experimental import pallas as pl
from jax.experimental.pallas import tpu as pltpu

F32 = jnp.float32
BF16 = jnp.bfloat16

D_MODEL = 1024
DEPTH = 4
CHUNK = 128
GM_WIDTH = 512
GM_HEADS = 4
MLA_HEADS = 8
QK_NOPE = 64
QK_ROPE = 32
V_HEAD = 64
Q_LORA = 384
KV_LORA = 256
ROPE_THETA = 10000.0
C_HEADS = 8
C_HEAD_DIM = 128
N_EXPERTS = 16
N_GROUPS = 4
EXPERTS_PER_GROUP = 4
TOP_K = 2
D_EXPERT = 512
ALPHA = (2 * DEPTH) ** 0.25
EPS = 1e-5
LANES = 128

VMEM_LIMIT = 56 * 1024 * 1024


def _cparams(sem):
    return pltpu.CompilerParams(dimension_semantics=sem, vmem_limit_bytes=VMEM_LIMIT)


def _layer_norm(y, g, b):
    mu = jnp.mean(y, axis=-1, keepdims=True)
    yc = y - mu
    var = jnp.mean(yc * yc, axis=-1, keepdims=True)
    return yc * lax.rsqrt(var + EPS) * g + b


def _rms_norm(y, g):
    return y * lax.rsqrt(jnp.mean(y * y, axis=-1, keepdims=True) + EPS) * g


def _full(shape):
    return pl.BlockSpec(shape, lambda *_: (0,) * len(shape))


EVEN_TM = 512


def _even_in_kernel(x_ref, tab_ref, wa_ref, gmg_ref, gmb_ref, wsp_ref, bsp_ref, qg_ref, wq_ref, kvg_ref, wkv_ref,
                    a_ref, q_ref, k_ref, v_ref):
    tm = x_ref.shape[0]
    z = jnp.dot(x_ref[...].astype(BF16), wa_ref[...], preferred_element_type=F32)
    ct = tab_ref[:, 0:LANES]
    st = tab_ref[:, LANES:2 * LANES]

    for g in range(GM_HEADS):
        lo = GM_WIDTH + g * LANES
        gv = jax.nn.gelu(z[:, lo:lo + LANES], approximate=True)
        gv = _layer_norm(gv, gmg_ref[:, g * LANES:(g + 1) * LANES], gmb_ref[:, g * LANES:(g + 1) * LANES]).astype(BF16)
        gu = jax.nn.gelu(z[:, g * LANES:(g + 1) * LANES], approximate=True)
        for c in range(tm // CHUNK):
            rows = slice(c * CHUNK, (c + 1) * CHUNK)
            sv = jnp.dot(wsp_ref[g], gv[rows], preferred_element_type=F32) + bsp_ref[g]
            a_ref[rows, g * LANES:(g + 1) * LANES] = (gu[rows] * sv).astype(a_ref.dtype)

    cq = _rms_norm(z[:, 2 * GM_WIDTH:2 * GM_WIDTH + Q_LORA], qg_ref[...]).astype(BF16)
    qq = jnp.dot(cq, wq_ref[...], preferred_element_type=F32)
    scale = (QK_NOPE + QK_ROPE) ** -0.5
    for h in range(MLA_HEADS):
        q1 = qq[:, h * LANES:(h + 1) * LANES]
        q2 = qq[:, D_MODEL + h * LANES:D_MODEL + (h + 1) * LANES]
        q_ref[:, h * LANES:(h + 1) * LANES] = ((q1 * ct + q2 * st) * scale).astype(q_ref.dtype)

    base = 2 * GM_WIDTH + Q_LORA
    ckv = _rms_norm(z[:, base:base + KV_LORA], kvg_ref[...]).astype(BF16)
    kv = jnp.dot(ckv, wkv_ref[...], preferred_element_type=F32)
    base += KV_LORA
    kr = z[:, base:base + LANES] * ct + z[:, base + LANES:base + 2 * LANES] * st
    for h in range(MLA_HEADS):
        k_ref[:, h * LANES:(h + 1) * LANES] = (kv[:, h * LANES:(h + 1) * LANES] + kr).astype(k_ref.dtype)
    v_ref[...] = kv[:, D_MODEL:].astype(v_ref.dtype)


def _even_in(x, tab, wts, n_prompt_rows, s_prompt, s_sample):
    t = x.shape[0]
    tm = EVEN_TM
    wa, gmg, gmb, wsp, bsp, qg, wq, kvg, wkv = wts

    def tab_map(i):
        return (jnp.where(i < n_prompt_rows // tm, i % (s_prompt // tm), i % (s_sample // tm)), 0)

    row = lambda w: pl.BlockSpec((tm, w), lambda i: (i, 0))
    return pl.pallas_call(
        _even_in_kernel,
        grid=(t // tm,),
        in_specs=[row(D_MODEL), pl.BlockSpec((tm, 2 * LANES), tab_map), _full(wa.shape), _full(gmg.shape),
                  _full(gmb.shape), _full(wsp.shape), _full(bsp.shape), _full(qg.shape), _full(wq.shape),
                  _full(kvg.shape), _full(wkv.shape)],
        out_specs=[row(GM_WIDTH), row(D_MODEL), row(D_MODEL), row(MLA_HEADS * V_HEAD)],
        out_shape=[jax.ShapeDtypeStruct((t, GM_WIDTH), BF16), jax.ShapeDtypeStruct((t, D_MODEL), BF16),
                   jax.ShapeDtypeStruct((t, D_MODEL), BF16), jax.ShapeDtypeStruct((t, MLA_HEADS * V_HEAD), BF16)],
        compiler_params=_cparams(("parallel",)),
        name="even_in",
    )(x, tab, wa, gmg, gmb, wsp, bsp, qg, wq, kvg, wkv)


ATT_TQ = 256
ATT_TK = 512


def _attn_kernel(q_ref, k_ref, v_ref, o_ref, *, seq, tk):
    tq = q_ref.shape[0]
    outs = []
    for j in range(2):
        q = q_ref[:, j * LANES:(j + 1) * LANES]

        def step(c, carry, j=j, q=q):
            m, l, acc = carry
            r0 = pl.multiple_of(c * tk, tk)
            kc = k_ref[pl.ds(r0, tk), j * LANES:(j + 1) * LANES]
            vc = v_ref[pl.ds(r0, tk), :]
            s = lax.dot_general(q, kc, (((1,), (1,)), ((), ())), preferred_element_type=F32)
            m_new = jnp.maximum(m, jnp.max(s, axis=-1, keepdims=True))
            p = jnp.exp(s - m_new)
            a = jnp.exp(m - m_new)
            l = a * l + jnp.sum(p, axis=-1, keepdims=True)
            acc = a * acc + jnp.dot(p.astype(BF16), vc, preferred_element_type=F32)
            return m_new, l, acc

        init = (jnp.full((tq, 1), -jnp.inf, F32), jnp.zeros((tq, 1), F32), jnp.zeros((tq, LANES), F32))
        _, l, acc = lax.fori_loop(0, seq // tk, step, init)
        outs.append(acc / l)
    lane = lax.broadcasted_iota(jnp.int32, (tq, LANES), 1)
    o_ref[...] = jnp.where(lane < V_HEAD, outs[0], outs[1]).astype(o_ref.dtype)


def _attention(q, k, v, row0, batch, seq):
    tq = ATT_TQ
    nq = seq // tq
    return pl.pallas_call(
        functools.partial(_attn_kernel, seq=seq, tk=ATT_TK),
        grid=(batch, MLA_HEADS // 2, nq),
        in_specs=[pl.BlockSpec((tq, 2 * LANES), lambda b, h, i: (row0 // tq + b * nq + i, h)),
                  pl.BlockSpec((seq, 2 * LANES), lambda b, h, i: (row0 // seq + b, h)),
                  pl.BlockSpec((seq, LANES), lambda b, h, i: (row0 // seq + b, h))],
        out_specs=pl.BlockSpec((tq, LANES), lambda b, h, i: (b * nq + i, h)),
        out_shape=jax.ShapeDtypeStruct((batch * seq, MLA_HEADS * V_HEAD), BF16),
        compiler_params=_cparams(("parallel", "parallel", "parallel")),
        name="attention",
    )(q, k, v)


OUT_TM = 512


def _post_mix(mix, x_ref, lg_ref, lb_ref, wr_ref, x1_ref, lgt_ref):
    x1 = _layer_norm(ALPHA * x_ref[...] + mix, lg_ref[...], lb_ref[...])
    x1_ref[...] = x1
    hi = x1.astype(BF16)
    lo = (x1 - hi.astype(F32)).astype(BF16)
    lgt_ref[...] = jnp.dot(jnp.concatenate([hi, lo], axis=1), wr_ref[...], preferred_element_type=F32)


def _even_out_kernel(a_ref, o_ref, x_ref, w_ref, lg_ref, lb_ref, wr_ref, x1_ref, lgt_ref):
    mix = jnp.dot(a_ref[...], w_ref[0:GM_WIDTH, :], preferred_element_type=F32)
    mix = mix + jnp.dot(o_ref[...], w_ref[GM_WIDTH:, :], preferred_element_type=F32)
    _post_mix(mix, x_ref, lg_ref, lb_ref, wr_ref, x1_ref, lgt_ref)


def _odd_out_kernel(hf_ref, hb_ref, og_ref, x_ref, ng_ref, nb_ref, w_ref, lg_ref, lb_ref, wr_ref, x1_ref, lgt_ref):
    parts = []
    for h in range(C_HEADS):
        cols = slice(h * LANES, (h + 1) * LANES)
        hs = hf_ref[:, cols].astype(F32) + hb_ref[:, cols].astype(F32)
        parts.append((_layer_norm(hs, ng_ref[:, cols], nb_ref[:, cols]) * og_ref[:, cols].astype(F32)).astype(BF16))
    mix = jnp.dot(jnp.concatenate(parts, axis=1), w_ref[...], preferred_element_type=F32)
    _post_mix(mix, x_ref, lg_ref, lb_ref, wr_ref, x1_ref, lgt_ref)


def _mixer_out(body, acts, x, consts):
    t = x.shape[0]
    tm = OUT_TM
    row = lambda w: pl.BlockSpec((tm, w), lambda i: (i, 0))
    return pl.pallas_call(
        body,
        grid=(t // tm,),
        in_specs=[row(a.shape[1]) for a in acts] + [row(D_MODEL)] + [_full(c.shape) for c in consts],
        out_specs=[row(D_MODEL), row(LANES)],
        out_shape=[jax.ShapeDtypeStruct((t, D_MODEL), F32), jax.ShapeDtypeStruct((t, LANES), F32)],
        compiler_params=_cparams(("parallel",)),
        name=body.__name__.strip("_"),
    )(*acts, x, *consts)


ODD_TM = 512


def _odd_in_kernel(x_ref, w_ref, wg_ref, bg_ref, q_ref, k_ref, v_ref, og_ref, g_ref):
    xb = x_ref[...].astype(BF16)
    q_ref[...] = jnp.dot(xb, w_ref[:, 0:D_MODEL], preferred_element_type=F32).astype(q_ref.dtype)
    k = jnp.dot(xb, w_ref[:, D_MODEL:2 * D_MODEL], preferred_element_type=F32)
    k_ref[...] = (k * (C_HEAD_DIM ** -0.5)).astype(k_ref.dtype)
    v_ref[...] = jnp.dot(xb, w_ref[:, 2 * D_MODEL:3 * D_MODEL], preferred_element_type=F32).astype(v_ref.dtype)
    o = jnp.dot(xb, w_ref[:, 3 * D_MODEL:4 * D_MODEL], preferred_element_type=F32)
    og_ref[...] = jax.nn.sigmoid(o).astype(og_ref.dtype)
    g = jnp.dot(xb, wg_ref[...], preferred_element_type=F32) + bg_ref[...]
    lane = lax.broadcasted_iota(jnp.int32, g.shape, 1)
    is_f = ((lane >= C_HEADS) & (lane < 2 * C_HEADS)) | ((lane >= 3 * C_HEADS) & (lane < 4 * C_HEADS))
    g_ref[...] = jnp.where(is_f, jax.nn.log_sigmoid(g), g)


def _odd_in(x, w, wg, bg):
    t = x.shape[0]
    tm = ODD_TM
    row = lambda w_: pl.BlockSpec((tm, w_), lambda i: (i, 0))
    big = jax.ShapeDtypeStruct((t, D_MODEL), BF16)
    return pl.pallas_call(
        _odd_in_kernel,
        grid=(t // tm,),
        in_specs=[row(D_MODEL), _full(w.shape), _full(wg.shape), _full(bg.shape)],
        out_specs=[row(D_MODEL)] * 4 + [row(LANES)],
        out_shape=[big, big, big, big, jax.ShapeDtypeStruct((t, LANES), F32)],
        compiler_params=_cparams(("parallel",)),
        name="odd_in",
    )(x, w, wg, bg)


def _mlstm_kernel(qf_ref, kf_ref, vf_ref, gf_ref, qb_ref, kb_ref, vb_ref, gb_ref, hf_ref, hb_ref, c_scr, m_scr):
    L = CHUNK

    @pl.when(pl.program_id(1) == 0)
    def _():
        c_scr[...] = jnp.zeros_like(c_scr)
        m_scr[...] = jnp.zeros_like(m_scr)

    row_i = lax.broadcasted_iota(jnp.int32, (L, L), 0)
    col_i = lax.broadcasted_iota(jnp.int32, (L, L), 1)
    ones_col = jnp.where(col_i == 0, 1.0, 0.0).astype(BF16)

    for d, (q_ref, k_ref, v_ref, g_ref, h_ref) in enumerate(
            ((qf_ref, kf_ref, vf_ref, gf_ref, hf_ref), (qb_ref, kb_ref, vb_ref, gb_ref, hb_ref))):
        past = (col_i <= row_i) if d == 0 else (col_i >= row_i)
        gates = g_ref[...]
        csum = jnp.dot(past.astype(F32), gates, preferred_element_type=F32, precision=lax.Precision.HIGHEST)
        gates_t = gates.T
        csum_t = csum.T
        i_col0 = 2 * C_HEADS * d
        f_col0 = i_col0 + C_HEADS
        last = L - 1 if d == 0 else 0
        for h in range(C_HEADS):
            cols = slice(h * LANES, (h + 1) * LANES)
            q = q_ref[:, cols]
            k = k_ref[:, cols]
            v_ext = jnp.concatenate([v_ref[:, cols], ones_col], axis=1)
            b = csum[:, f_col0 + h:f_col0 + h + 1]
            ic = gates[:, i_col0 + h:i_col0 + h + 1]
            r_row = gates_t[i_col0 + h:i_col0 + h + 1, :] - csum_t[f_col0 + h:f_col0 + h + 1, :]
            sd = 2 * h + d
            m_prev = m_scr[sd:sd + 1, 0:1]
            c_ext = c_scr[sd]

            dmat = jnp.where(past, b + r_row, -jnp.inf)
            inter = b + m_prev
            m_t = jnp.maximum(inter, jnp.max(dmat, axis=-1, keepdims=True))
            w_inter = jnp.exp(inter - m_t)
            qk = lax.dot_general(q, k, (((1,), (1,)), ((), ())), preferred_element_type=F32)
            att = (jnp.exp(dmat - m_t) * qk).astype(BF16)
            num = w_inter * jnp.dot(q, c_ext.astype(BF16), preferred_element_type=F32)
            num = num + jnp.dot(att, v_ext, preferred_element_type=F32)
            den = num[:, LANES:LANES + 1]
            h_ref[:, cols] = (num[:, 0:LANES] / jnp.maximum(jnp.abs(den), jnp.exp(-m_t))).astype(h_ref.dtype)

            b_last = b[last:last + 1, :]
            g_col = b_last - b + ic
            m_new = jnp.maximum(b_last + m_prev, jnp.max(g_col, axis=0, keepdims=True))
            decay = jnp.exp(b_last + m_prev - m_new)
            kw = (jnp.exp(g_col - m_new) * k.astype(F32)).astype(BF16)
            upd = lax.dot_general(kw, v_ext, (((0,), (0,)), ((), ())), preferred_element_type=F32)
            c_scr[sd] = decay * c_ext + upd
            m_scr[sd:sd + 1, :] = jnp.broadcast_to(m_new, (1, LANES))


def _mlstm(q, k, v, g, row0, batch, seq):
    nc = seq // CHUNK
    c0 = row0 // CHUNK
    fwd = lambda b, c: (c0 + b * nc + c, 0)
    bwd = lambda b, c: (c0 + b * nc + nc - 1 - c, 0)
    big = lambda m: pl.BlockSpec((CHUNK, D_MODEL), m)
    small = lambda m: pl.BlockSpec((CHUNK, LANES), m)
    out = jax.ShapeDtypeStruct((batch * seq, D_MODEL), BF16)
    return pl.pallas_call(
        _mlstm_kernel,
        grid=(batch, nc),
        in_specs=[big(fwd), big(fwd), big(fwd), small(fwd), big(bwd), big(bwd), big(bwd), small(bwd)],
        out_specs=[big(lambda b, c: (b * nc + c, 0)), big(lambda b, c: (b * nc + nc - 1 - c, 0))],
        out_shape=[out, out],
        scratch_shapes=[pltpu.VMEM((2 * C_HEADS, CHUNK, 2 * LANES), F32), pltpu.VMEM((2 * C_HEADS, LANES), F32)],
        compiler_params=_cparams(("parallel", "arbitrary")),
        name="mlstm",
    )(q, k, v, g, q, k, v, g)


MOE_BM = 256
DISPATCH_TM = 256


def _dispatch_kernel(dest_ref, x_ref, xs_hbm, sem):
    tm = x_ref.shape[0]

    def issue(t, c):
        for kk in range(TOP_K):
            d = dest_ref[0, 0, TOP_K * t + kk]
            pltpu.make_async_copy(x_ref.at[pl.ds(t, 1), :], xs_hbm.at[pl.ds(d, 1), :], sem).start()
        return c

    lax.fori_loop(0, tm, issue, 0, unroll=8)
    for _ in range(TOP_K):
        pltpu.make_async_copy(x_ref, xs_hbm.at[pl.ds(0, tm), :], sem).wait()


def _dispatch(x1, dest, n_slots):
    t = x1.shape[0]
    tm = DISPATCH_TM
    return pl.pallas_call(
        _dispatch_kernel,
        grid=(t // tm,),
        in_specs=[pl.BlockSpec((1, 1, TOP_K * tm), lambda i: (i, 0, 0), memory_space=pltpu.SMEM),
                  pl.BlockSpec((tm, D_MODEL), lambda i: (i, 0))],
        out_specs=pl.BlockSpec(memory_space=pl.ANY),
        out_shape=jax.ShapeDtypeStruct((n_slots, D_MODEL), F32),
        scratch_shapes=[pltpu.SemaphoreType.DMA(())],
        compiler_params=_cparams(("arbitrary",)),
        name="moe_dispatch",
    )(dest.reshape(t // tm, 1, TOP_K * tm), x1)


def _experts_kernel(be_ref, nv_ref, xs_ref, wg_ref, wu_ref, wd_ref, ys_ref):
    i = pl.program_id(0)
    nv = nv_ref[i]

    @pl.when(nv > 0)
    def _():
        rows = lax.broadcasted_iota(jnp.int32, xs_ref.shape, 0)
        xb = jnp.where(rows < nv, xs_ref[...], 0.0).astype(BF16)
        hg = jnp.dot(xb, wg_ref[...], preferred_element_type=F32)
        hu = jnp.dot(xb, wu_ref[...], preferred_element_type=F32)
        hh = (jax.nn.silu(hg) * hu).astype(BF16)
        ys_ref[...] = jnp.dot(hh, wd_ref[...], preferred_element_type=F32)

    @pl.when(nv == 0)
    def _():
        ys_ref[...] = jnp.zeros_like(ys_ref)


def _experts(xs, block_e, n_valid, wg, wu, wd):
    n_slots = xs.shape[0]
    bm = MOE_BM
    wspec = lambda s: pl.BlockSpec((None,) + s, lambda i, be, nv: (be[i], 0, 0))
    return pl.pallas_call(
        _experts_kernel,
        grid_spec=pltpu.PrefetchScalarGridSpec(
            num_scalar_prefetch=2,
            grid=(n_slots // bm,),
            in_specs=[pl.BlockSpec((bm, D_MODEL), lambda i, be, nv: (i, 0)),
                      wspec((D_MODEL, D_EXPERT)), wspec((D_MODEL, D_EXPERT)), wspec((D_EXPERT, D_MODEL))],
            out_specs=pl.BlockSpec((bm, D_MODEL), lambda i, be, nv: (i, 0)),
        ),
        out_shape=jax.ShapeDtypeStruct((n_slots, D_MODEL), F32),
        compiler_params=_cparams(("arbitrary",)),
        name="moe_experts",
    )(block_e, n_valid, xs, wg, wu, wd)


def _combine_kernel(dest_ref, ys_hbm, x_ref, gate_ref, lg_ref, lb_ref, o_ref, ybuf, sem):
    tm = x_ref.shape[0]

    def issue(t, c):
        for kk in range(TOP_K):
            d = dest_ref[0, 0, TOP_K * t + kk]
            pltpu.make_async_copy(ys_hbm.at[pl.ds(d, 1), :], ybuf.at[kk, pl.ds(t, 1), :], sem).start()
        return c

    lax.fori_loop(0, tm, issue, 0, unroll=8)
    for kk in range(TOP_K):
        pltpu.make_async_copy(ys_hbm.at[pl.ds(0, tm), :], ybuf.at[kk], sem).wait()
    y = gate_ref[:, 0:1] * ybuf[0] + gate_ref[:, 1:2] * ybuf[1]
    o_ref[...] = _layer_norm(ALPHA * x_ref[...] + y, lg_ref[...], lb_ref[...])


def _combine(ys, dest, x1, gate, lg, lb):
    t = x1.shape[0]
    tm = DISPATCH_TM
    return pl.pallas_call(
        _combine_kernel,
        grid=(t // tm,),
        in_specs=[pl.BlockSpec((1, 1, TOP_K * tm), lambda i: (i, 0, 0), memory_space=pltpu.SMEM),
                  pl.BlockSpec(memory_space=pl.ANY),
                  pl.BlockSpec((tm, D_MODEL), lambda i: (i, 0)),
                  pl.BlockSpec((tm, TOP_K), lambda i: (i, 0)),
                  _full(lg.shape), _full(lb.shape)],
        out_specs=pl.BlockSpec((tm, D_MODEL), lambda i: (i, 0)),
        out_shape=jax.ShapeDtypeStruct((t, D_MODEL), F32),
        scratch_shapes=[pltpu.VMEM((TOP_K, tm, D_MODEL), F32), pltpu.SemaphoreType.DMA(())],
        compiler_params=_cparams(("arbitrary",)),
        name="moe_combine",
    )(dest.reshape(t // tm, 1, TOP_K * tm), ys, x1, gate, lg, lb)


def _route(lgt, router_bias):
    t = lgt.shape[0]
    logits = lgt[:, 0:N_EXPERTS] + lgt[:, N_EXPERTS:2 * N_EXPERTS]
    scores = jax.nn.sigmoid(logits)
    biased = scores + router_bias.astype(F32)
    grp_score = lax.top_k(biased.reshape(t, N_GROUPS, EXPERTS_PER_GROUP), 2)[0].sum(-1)
    g_sel = jnp.argmax(grp_score, axis=-1)
    in_group = (jnp.arange(N_EXPERTS) // EXPERTS_PER_GROUP)[None, :] == g_sel[:, None]
    _, idx = lax.top_k(jnp.where(in_group, biased, -jnp.inf), TOP_K)
    gate = jnp.take_along_axis(scores, idx, axis=-1)
    gate = gate / gate.sum(-1, keepdims=True)

    bm = MOE_BM
    flat_e = idx.reshape(-1)
    onehot = (flat_e[:, None] == jnp.arange(N_EXPERTS)[None, :]).astype(jnp.int32)
    csum = jnp.cumsum(onehot, axis=0)
    rank = jnp.sum(csum * onehot, axis=1) - 1
    counts = csum[-1]
    padded = (counts + bm - 1) // bm * bm
    pad_end = jnp.cumsum(padded)
    pad_start = pad_end - padded
    dest = (jnp.sum(pad_start[None, :] * onehot, axis=1) + rank).astype(jnp.int32)
    n_blocks = (t * TOP_K) // bm + N_EXPERTS
    blk0 = jnp.arange(n_blocks, dtype=jnp.int32) * bm
    block_e = jnp.minimum(jnp.sum(blk0[:, None] >= pad_end[None, :], axis=1), N_EXPERTS - 1).astype(jnp.int32)
    seg_end = (pad_start + counts)[block_e]
    n_valid = jnp.clip(seg_end - blk0, 0, bm).astype(jnp.int32)
    n_valid = jnp.where(blk0 < pad_end[-1], n_valid, 0)
    return dest.reshape(t, TOP_K), gate, block_e, n_valid, n_blocks * bm


def _moe(x1, lgt, router_bias, wg, wu, wd, lg, lb):
    dest, gate, block_e, n_valid, n_slots = _route(lgt, router_bias)
    xs = _dispatch(x1, dest, n_slots)
    ys = _experts(xs, block_e, n_valid, wg, wu, wd)
    return _combine(ys, dest, x1, gate, lg, lb)


def _rope_table(seq):
    inv = jnp.power(ROPE_THETA, -jnp.arange(0, QK_ROPE, 2, dtype=F32) / QK_ROPE)
    ang = jnp.arange(seq, dtype=F32)[:, None] * inv[None, :]
    cos, sin = jnp.cos(ang), jnp.sin(ang)
    one = jnp.ones((seq, QK_NOPE), F32)
    z64 = jnp.zeros((seq, QK_NOPE), F32)
    z32 = jnp.zeros((seq, LANES - QK_NOPE - QK_ROPE), F32)
    return jnp.concatenate([one, cos, cos, z32, z64, sin, sin, z32], axis=1)


def _swap_pairs(w):
    half = QK_ROPE // 2
    return jnp.concatenate([-w[..., half:], w[..., :half]], axis=-1)


def _even_weights(w_in, gm_g, gm_b, w_sp, b_sp, q_g, w_uq, kv_g, w_ukv):
    o0, o1, o2, o3 = GM_WIDTH, 2 * GM_WIDTH, 2 * GM_WIDTH + Q_LORA, 2 * GM_WIDTH + Q_LORA + KV_LORA
    w_kr = w_in[:, o3:]
    z64 = jnp.zeros((D_MODEL, QK_NOPE), F32)
    z32 = jnp.zeros((D_MODEL, LANES - QK_NOPE - QK_ROPE), F32)
    wa = jnp.concatenate([w_in[:, :o3], z64, w_kr, z32, z64, _swap_pairs(w_kr), z32], axis=1).astype(BF16)
    wq = w_uq.reshape(Q_LORA, MLA_HEADS, QK_NOPE + QK_ROPE)
    nope, rope = wq[..., :QK_NOPE], wq[..., QK_NOPE:]
    zq32 = jnp.zeros((Q_LORA, MLA_HEADS, LANES - QK_NOPE - QK_ROPE), F32)
    zq64 = jnp.zeros((Q_LORA, MLA_HEADS, QK_NOPE), F32)
    q1 = jnp.concatenate([nope, rope, zq32], axis=-1).reshape(Q_LORA, D_MODEL)
    q2 = jnp.concatenate([zq64, _swap_pairs(rope), zq32], axis=-1).reshape(Q_LORA, D_MODEL)
    wq = jnp.concatenate([q1, q2], axis=1).astype(BF16)
    wkv = w_ukv.reshape(KV_LORA, MLA_HEADS, QK_NOPE + V_HEAD)
    kn = jnp.concatenate([wkv[..., :QK_NOPE], jnp.zeros((KV_LORA, MLA_HEADS, LANES - QK_NOPE), F32)], axis=-1)
    wkv = jnp.concatenate([kn.reshape(KV_LORA, D_MODEL), wkv[..., QK_NOPE:].reshape(KV_LORA, MLA_HEADS * V_HEAD)],
                          axis=1).astype(BF16)
    bsp = jnp.broadcast_to(b_sp[:, :, None], (GM_HEADS, CHUNK, LANES)).astype(F32)
    return (wa, gm_g.reshape(1, GM_WIDTH), gm_b.reshape(1, GM_WIDTH), w_sp.astype(BF16), bsp,
            q_g.reshape(1, Q_LORA), wq, kv_g.reshape(1, KV_LORA), wkv)


def _router_weights(router_w):
    whi = router_w.astype(BF16)
    wlo = (router_w - whi.astype(F32)).astype(BF16)
    pad = jnp.zeros((D_MODEL, LANES - 2 * N_EXPERTS), BF16)
    top = jnp.concatenate([whi, wlo, pad], axis=1)
    bot = jnp.concatenate([whi, jnp.zeros_like(wlo), pad], axis=1)
    return jnp.concatenate([top, bot], axis=0)


def kernel(x_prompt, x_sample, even_w_in, gm_norm_g, gm_norm_b, gm_w_spatial, gm_b_spatial, mla_q_norm_g, mla_w_uq, mla_kv_norm_g, mla_w_ukv, even_w_out, mlstm_w_in, mlstm_b_gates, mlstm_norm_g, mlstm_norm_b, mlstm_w_out, router_w, router_bias, moe_w_gate, moe_w_up, moe_w_down, ln_mix_g, ln_mix_b, ln_ffn_g, ln_ffn_b):
    bp, sp, _ = x_prompt.shape
    bs, ss, _ = x_sample.shape
    n_prompt = bp * sp
    x = jnp.concatenate([x_prompt.reshape(n_prompt, D_MODEL), x_sample.reshape(bs * ss, D_MODEL)], axis=0)
    groups = ((0, bp, sp), (n_prompt, bs, ss))
    tab = _rope_table(max(sp, ss))
    wr = _router_weights(router_w)
    row = lambda a: a.reshape(1, -1).astype(F32)

    for l in range(DEPTH):
        j = l // 2
        if l % 2 == 0:
            wts = _even_weights(even_w_in[j], gm_norm_g[j], gm_norm_b[j], gm_w_spatial[j], gm_b_spatial[j],
                                mla_q_norm_g[j], mla_w_uq[j], mla_kv_norm_g[j], mla_w_ukv[j])
            a, q, k, v = _even_in(x, tab, wts, n_prompt, sp, ss)
            o = jnp.concatenate([_attention(q, k, v, r0, b, s) for r0, b, s in groups], axis=0)
            consts = (even_w_out[j].astype(BF16), row(ln_mix_g[l]), row(ln_mix_b[l]), wr)
            x1, lgt = _mixer_out(_even_out_kernel, (a, o), x, consts)
        else:
            w_in = mlstm_w_in[j]
            wg = jnp.concatenate([w_in[:, 4 * D_MODEL:], jnp.zeros((D_MODEL, LANES - 4 * C_HEADS), F32)], axis=1)
            bg = jnp.concatenate([mlstm_b_gates[j], jnp.zeros((LANES - 4 * C_HEADS,), F32)]).reshape(1, LANES)
            q, k, v, og, g = _odd_in(x, w_in[:, :4 * D_MODEL].astype(BF16), wg.astype(BF16), bg)
            hs = [_mlstm(q, k, v, g, r0, b, s) for r0, b, s in groups]
            hf = jnp.concatenate([h[0] for h in hs], axis=0)
            hb = jnp.concatenate([h[1] for h in hs], axis=0)
            consts = (row(mlstm_norm_g[j]), row(mlstm_norm_b[j]), mlstm_w_out[j].astype(BF16),
                      row(ln_mix_g[l]), row(ln_mix_b[l]), wr)
            x1, lgt = _mixer_out(_odd_out_kernel, (hf, hb, og), x, consts)
        x = _moe(x1, lgt, router_bias, moe_w_gate[l].astype(BF16), moe_w_up[l].astype(BF16),
                 moe_w_down[l].astype(BF16), row(ln_ffn_g[l]), row(ln_ffn_b[l]))

    return x[:n_prompt].reshape(bp, sp, D_MODEL), x[n_prompt:].reshape(bs, ss, D_MODEL)
```

```python
import functools

import jax
import jax.numpy as jnp
from jax import lax
from jax.experimental import pallas as pl
from jax.experimental.pallas import tpu as pltpu

F32 = jnp.float32
BF16 = jnp.bfloat16

D_MODEL = 1024
DEPTH = 4
CHUNK = 128
GM_WIDTH = 512
GM_HEADS = 4
MLA_HEADS = 8
QK_NOPE = 64
QK_ROPE = 32
V_HEAD = 64
Q_LORA = 384
KV_LORA = 256
ROPE_THETA = 10000.0
C_HEADS = 8
C_HEAD_DIM = 128
N_EXPERTS = 16
N_GROUPS = 4
EXPERTS_PER_GROUP = 4
TOP_K = 2
D_EXPERT = 512
ALPHA = (2 * DEPTH) ** 0.25
EPS = 1e-5
LOG2_E = 1.4426950408889634
LANES = 128

VMEM_LIMIT = 56 * 1024 * 1024


def _cparams(sem):
    return pltpu.CompilerParams(dimension_semantics=sem, vmem_limit_bytes=VMEM_LIMIT)


def _layer_norm(y, g, b):
    mu = jnp.mean(y, axis=-1, keepdims=True)
    yc = y - mu
    var = jnp.mean(yc * yc, axis=-1, keepdims=True)
    return yc * lax.rsqrt(var + EPS) * g + b


def _rms_norm(y, g):
    return y * lax.rsqrt(jnp.mean(y * y, axis=-1, keepdims=True) + EPS) * g


def _full(shape):
    return pl.BlockSpec(shape, lambda *_: (0,) * len(shape))


EVEN_TM = 512


def _even_in_kernel(x_ref, tab_ref, wa_ref, gmg_ref, gmb_ref, wsp_ref, bsp_ref, qg_ref, wq_ref, kvg_ref, wkv_ref,
                    a_ref, q_ref, k_ref, vt_ref):
    tm = x_ref.shape[0]
    z = jnp.dot(x_ref[...].astype(BF16), wa_ref[...], preferred_element_type=F32)
    ct = tab_ref[:, 0:LANES]
    st = tab_ref[:, LANES:2 * LANES]

    for g in range(GM_HEADS):
        lo = GM_WIDTH + g * LANES
        gv = jax.nn.gelu(z[:, lo:lo + LANES], approximate=True)
        gv = _layer_norm(gv, gmg_ref[:, g * LANES:(g + 1) * LANES], gmb_ref[:, g * LANES:(g + 1) * LANES]).astype(BF16)
        gu = jax.nn.gelu(z[:, g * LANES:(g + 1) * LANES], approximate=True)
        for c in range(tm // CHUNK):
            rows = slice(c * CHUNK, (c + 1) * CHUNK)
            sv = jnp.dot(wsp_ref[g], gv[rows], preferred_element_type=F32) + bsp_ref[g]
            a_ref[rows, g * LANES:(g + 1) * LANES] = (gu[rows] * sv).astype(a_ref.dtype)

    cq = _rms_norm(z[:, 2 * GM_WIDTH:2 * GM_WIDTH + Q_LORA], qg_ref[...]).astype(BF16)
    qq = jnp.dot(cq, wq_ref[...], preferred_element_type=F32)
    scale = (QK_NOPE + QK_ROPE) ** -0.5 * LOG2_E
    for h in range(MLA_HEADS):
        q1 = qq[:, h * LANES:(h + 1) * LANES]
        q2 = qq[:, D_MODEL + h * LANES:D_MODEL + (h + 1) * LANES]
        q_ref[:, h * LANES:(h + 1) * LANES] = ((q1 * ct + q2 * st) * scale).astype(q_ref.dtype)

    base = 2 * GM_WIDTH + Q_LORA
    ckv = _rms_norm(z[:, base:base + KV_LORA], kvg_ref[...]).astype(BF16)
    kv = jnp.dot(ckv, wkv_ref[...], preferred_element_type=F32)
    base += KV_LORA
    kr = z[:, base:base + LANES] * ct + z[:, base + LANES:base + 2 * LANES] * st
    for h in range(MLA_HEADS):
        k_ref[:, h * LANES:(h + 1) * LANES] = (kv[:, h * LANES:(h + 1) * LANES] + kr).astype(k_ref.dtype)
    vt_ref[...] = kv[:, D_MODEL:].T.astype(vt_ref.dtype)


def _even_in(x, tab, wts, n_prompt_rows, s_prompt, s_sample):
    t = x.shape[0]
    tm = EVEN_TM
    wa, gmg, gmb, wsp, bsp, qg, wq, kvg, wkv = wts

    def tab_map(i):
        return (jnp.where(i < n_prompt_rows // tm, i % (s_prompt // tm), i % (s_sample // tm)), 0)

    row = lambda w: pl.BlockSpec((tm, w), lambda i: (i, 0))
    return pl.pallas_call(
        _even_in_kernel,
        grid=(t // tm,),
        in_specs=[row(D_MODEL), pl.BlockSpec((tm, 2 * LANES), tab_map), _full(wa.shape), _full(gmg.shape),
                  _full(gmb.shape), _full(wsp.shape), _full(bsp.shape), _full(qg.shape), _full(wq.shape),
                  _full(kvg.shape), _full(wkv.shape)],
        out_specs=[row(GM_WIDTH), row(D_MODEL), row(D_MODEL),
                   pl.BlockSpec((MLA_HEADS * V_HEAD, tm), lambda i: (0, i))],
        out_shape=[jax.ShapeDtypeStruct((t, GM_WIDTH), BF16), jax.ShapeDtypeStruct((t, D_MODEL), BF16),
                   jax.ShapeDtypeStruct((t, D_MODEL), BF16), jax.ShapeDtypeStruct((MLA_HEADS * V_HEAD, t), BF16)],
        compiler_params=_cparams(("parallel",)),
        name="even_in",
    )(x, tab, wa, gmg, gmb, wsp, bsp, qg, wq, kvg, wkv)


ATT_TQ = 256
ATT_TK = 512
ATT_SUM_ROWS = 16
ATT_UNROLL = 4


def _attn_kernel(q_ref, k_ref, vt_ref, o_ref, sa0, sa1, sb0, sb1, *, seq, tk):
    tq = q_ref.shape[0]
    n = seq // tk
    qs = [q_ref[:, j * LANES:(j + 1) * LANES] for j in range(2)]
    slots = ((sa0, sa1), (sb0, sb1))

    def scores(j, c):
        r0 = pl.multiple_of(jnp.minimum(c, n - 1) * tk, tk)
        kc = k_ref[pl.ds(r0, tk), j * LANES:(j + 1) * LANES]
        return lax.dot_general(kc, qs[j], (((1,), (1,)), ((), ())), preferred_element_type=F32)

    def consume(c, cur, nxt, carry):
        for j in range(2):
            nxt[j][...] = scores(j, c + 1)
        r0 = pl.multiple_of(c * tk, tk)
        new = []
        for j in range(2):
            m, acc = carry[j]
            st = cur[j][...]
            m_new = jnp.maximum(m, jnp.max(st, axis=0, keepdims=True))
            p = jnp.exp2((st - m_new).astype(BF16))
            a = jnp.exp2(m - m_new)
            vtc = jnp.concatenate([vt_ref[j * V_HEAD:(j + 1) * V_HEAD, pl.ds(r0, tk)], ones_rows], axis=0)
            acc = a * acc + jnp.dot(vtc, p, preferred_element_type=F32)
            new.append((m_new, acc))
        return tuple(new)

    def step(i, carry):
        for u in range(ATT_UNROLL):
            carry = consume(ATT_UNROLL * i + u, slots[u % 2], slots[(u + 1) % 2], carry)
        return carry

    ones_rows = jnp.ones((ATT_SUM_ROWS, tk), BF16)
    for j in range(2):
        slots[0][j][...] = scores(j, 0)
    init = tuple((jnp.full((1, tq), -jnp.inf, F32), jnp.zeros((V_HEAD + ATT_SUM_ROWS, tq), F32)) for _ in range(2))
    res = lax.fori_loop(0, n // ATT_UNROLL, step, init)
    ot = jnp.concatenate([acc[0:V_HEAD] / acc[V_HEAD:V_HEAD + 1] for _, acc in res], axis=0)
    o_ref[...] = ot.T.astype(o_ref.dtype)


def _attention(q, k, vt, row0, batch, seq):
    tq = ATT_TQ
    nq = seq // tq
    assert seq % (ATT_TK * ATT_UNROLL) == 0 and seq % tq == 0 and row0 % seq == 0
    return pl.pallas_call(
        functools.partial(_attn_kernel, seq=seq, tk=ATT_TK),
        grid=(batch, MLA_HEADS // 2, nq),
        in_specs=[pl.BlockSpec((tq, 2 * LANES), lambda b, h, i: (row0 // tq + b * nq + i, h)),
                  pl.BlockSpec((seq, 2 * LANES), lambda b, h, i: (row0 // seq + b, h)),
                  pl.BlockSpec((2 * V_HEAD, seq), lambda b, h, i: (h, row0 // seq + b))],
        out_specs=pl.BlockSpec((tq, LANES), lambda b, h, i: (b * nq + i, h)),
        out_shape=jax.ShapeDtypeStruct((batch * seq, MLA_HEADS * V_HEAD), BF16),
        scratch_shapes=[pltpu.VMEM((ATT_TK, tq), F32)] * 4,
        compiler_params=_cparams(("parallel", "parallel", "parallel")),
        name="attention",
    )(q, k, vt)


OUT_TM = 512
ROUTE_ROWS = 8


def _first_of(vals, target):
    idx = jnp.full(target.shape, len(vals) - 1, jnp.int32)
    for j in range(len(vals) - 2, -1, -1):
        idx = jnp.where(vals[j] == target, j, idx)
    return idx


def _pick(vals, idx):
    out = vals[-1]
    for j in range(len(vals) - 2, -1, -1):
        out = jnp.where(idx == j, vals[j], out)
    return out


def _route_rows(logits, bias_col):
    scores = jax.nn.sigmoid(logits)
    biased = scores + bias_col
    b = [biased[e:e + 1, :] for e in range(N_EXPERTS)]
    s = [scores[e:e + 1, :] for e in range(N_EXPERTS)]
    gs = []
    for g in range(N_GROUPS):
        v0, v1, v2, v3 = b[EXPERTS_PER_GROUP * g:EXPERTS_PER_GROUP * (g + 1)]
        hi1, lo1 = jnp.maximum(v0, v1), jnp.minimum(v0, v1)
        hi2, lo2 = jnp.maximum(v2, v3), jnp.minimum(v2, v3)
        gs.append(jnp.maximum(hi1, hi2) + jnp.maximum(jnp.minimum(hi1, hi2), jnp.maximum(lo1, lo2)))
    g_sel = _first_of(gs, functools.reduce(jnp.maximum, gs))
    v = [_pick([b[EXPERTS_PER_GROUP * g + j] for g in range(N_GROUPS)], g_sel) for j in range(EXPERTS_PER_GROUP)]
    sc = [_pick([s[EXPERTS_PER_GROUP * g + j] for g in range(N_GROUPS)], g_sel) for j in range(EXPERTS_PER_GROUP)]
    i1 = _first_of(v, functools.reduce(jnp.maximum, v))
    rest = [jnp.where(i1 == j, -jnp.inf, v[j]) for j in range(EXPERTS_PER_GROUP)]
    i2 = _first_of(rest, functools.reduce(jnp.maximum, rest))
    s1, s2 = _pick(sc, i1), _pick(sc, i2)
    den = s1 + s2
    return g_sel * EXPERTS_PER_GROUP + i1, g_sel * EXPERTS_PER_GROUP + i2, s1 / den, s2 / den


def _post_mix(mix, x_ref, lg_ref, lb_ref, wr_ref, rb_ref, su_ref, x1_ref, ri_ref, rf_ref, cnt_ref, carry_scr):
    @pl.when(pl.program_id(0) == 0)
    def _():
        carry_scr[...] = jnp.zeros_like(carry_scr)

    tm = x_ref.shape[0]
    x1 = _layer_norm(ALPHA * x_ref[...] + mix, lg_ref[...], lb_ref[...])
    x1_ref[...] = x1
    hi = x1.astype(BF16)
    lo = (x1 - hi.astype(F32)).astype(BF16)
    lgt = jnp.dot(jnp.concatenate([hi, lo], axis=1), wr_ref[...], preferred_element_type=F32).T
    e1, e2, g1, g2 = _route_rows(lgt[0:N_EXPERTS] + lgt[N_EXPERTS:2 * N_EXPERTS], rb_ref[:, 0:1])

    erow = lax.broadcasted_iota(jnp.int32, (N_EXPERTS, tm), 0)
    o1, o2 = erow == e1, erow == e2
    oh = jnp.where(o1 | o2, 1.0, 0.0)
    before = jnp.dot(oh.astype(BF16), su_ref[...], preferred_element_type=F32) + carry_scr[:, 0:1]
    r1 = jnp.sum(jnp.where(o1, before, 0.0), axis=0, keepdims=True).astype(jnp.int32)
    r2 = jnp.sum(jnp.where(o2, before, 0.0), axis=0, keepdims=True).astype(jnp.int32)
    carry_scr[...] = carry_scr[...] + jnp.sum(oh, axis=1, keepdims=True)
    cnt_ref[...] = carry_scr[...]
    ri_ref[...] = jnp.concatenate([e1, e2, r1, r2, jnp.zeros((ROUTE_ROWS - 4, tm), jnp.int32)], axis=0)
    rf_ref[...] = jnp.concatenate([g1, g2, jnp.zeros((ROUTE_ROWS - 2, tm), F32)], axis=0)


def _even_out_kernel(a_ref, op_ref, os_ref, x_ref, w_ref, *rest, prompt_tiles):
    o = jnp.where(pl.program_id(0) < prompt_tiles, op_ref[...], os_ref[...])
    mix = jnp.dot(a_ref[...], w_ref[0:GM_WIDTH, :], preferred_element_type=F32)
    mix = mix + jnp.dot(o, w_ref[GM_WIDTH:, :], preferred_element_type=F32)
    _post_mix(mix, x_ref, *rest)


def _odd_out_kernel(hfp_ref, hbp_ref, hfs_ref, hbs_ref, og_ref, x_ref, ng_ref, nb_ref, w_ref, *rest, prompt_tiles):
    is_prompt = pl.program_id(0) < prompt_tiles
    parts = []
    for h in range(C_HEADS):
        cols = slice(h * LANES, (h + 1) * LANES)
        hf = jnp.where(is_prompt, hfp_ref[:, cols], hfs_ref[:, cols])
        hb = jnp.where(is_prompt, hbp_ref[:, cols], hbs_ref[:, cols])
        hs = hf.astype(F32) + hb.astype(F32)
        parts.append((_layer_norm(hs, ng_ref[:, cols], nb_ref[:, cols]) * og_ref[:, cols].astype(F32)).astype(BF16))
    mix = jnp.dot(jnp.concatenate(parts, axis=1), w_ref[...], preferred_element_type=F32)
    _post_mix(mix, x_ref, *rest)


def _mixer_out(body, acts, x, consts, n_prompt_rows):
    t = x.shape[0]
    tm = OUT_TM
    pt = n_prompt_rows // tm
    maps = {None: lambda i: (i, 0), 0: lambda i: (jnp.minimum(i, pt - 1), 0), 1: lambda i: (jnp.maximum(i - pt, 0), 0)}
    row = lambda w: pl.BlockSpec((tm, w), lambda i: (i, 0))
    route_spec = pl.BlockSpec((ROUTE_ROWS, tm), lambda i: (0, i))
    return pl.pallas_call(
        functools.partial(body, prompt_tiles=pt),
        grid=(t // tm,),
        in_specs=[pl.BlockSpec((tm, a.shape[1]), maps[g]) for a, g in acts] + [row(D_MODEL)]
        + [_full(c.shape) for c in consts],
        out_specs=[row(D_MODEL), route_spec, route_spec, _full((N_EXPERTS, LANES))],
        out_shape=[jax.ShapeDtypeStruct((t, D_MODEL), F32), jax.ShapeDtypeStruct((ROUTE_ROWS, t), jnp.int32),
                   jax.ShapeDtypeStruct((ROUTE_ROWS, t), F32), jax.ShapeDtypeStruct((N_EXPERTS, LANES), F32)],
        scratch_shapes=[pltpu.VMEM((N_EXPERTS, LANES), F32)],
        compiler_params=_cparams(("arbitrary",)),
        name=body.__name__.strip("_"),
    )(*[a for a, _ in acts], x, *consts)


ODD_TM = 512


def _odd_in_kernel(x_ref, w_ref, wg_ref, bg_ref, q_ref, k_ref, v_ref, og_ref, g_ref):
    xb = x_ref[...].astype(BF16)
    q_ref[...] = jnp.dot(xb, w_ref[:, 0:D_MODEL], preferred_element_type=F32).astype(q_ref.dtype)
    k = jnp.dot(xb, w_ref[:, D_MODEL:2 * D_MODEL], preferred_element_type=F32)
    k_ref[...] = (k * (C_HEAD_DIM ** -0.5)).astype(k_ref.dtype)
    v_ref[...] = jnp.dot(xb, w_ref[:, 2 * D_MODEL:3 * D_MODEL], preferred_element_type=F32).astype(v_ref.dtype)
    o = jnp.dot(xb, w_ref[:, 3 * D_MODEL:4 * D_MODEL], preferred_element_type=F32)
    og_ref[...] = jax.nn.sigmoid(o).astype(og_ref.dtype)
    g = jnp.dot(xb, wg_ref[...], preferred_element_type=F32) + bg_ref[...]
    lane = lax.broadcasted_iota(jnp.int32, g.shape, 1)
    is_f = ((lane >= C_HEADS) & (lane < 2 * C_HEADS)) | ((lane >= 3 * C_HEADS) & (lane < 4 * C_HEADS))
    g_ref[...] = jnp.where(is_f, jax.nn.log_sigmoid(g), g)


def _odd_in(x, w, wg, bg):
    t = x.shape[0]
    tm = ODD_TM
    row = lambda w_: pl.BlockSpec((tm, w_), lambda i: (i, 0))
    big = jax.ShapeDtypeStruct((t, D_MODEL), BF16)
    return pl.pallas_call(
        _odd_in_kernel,
        grid=(t // tm,),
        in_specs=[row(D_MODEL), _full(w.shape), _full(wg.shape), _full(bg.shape)],
        out_specs=[row(D_MODEL)] * 4 + [row(LANES)],
        out_shape=[big, big, big, big, jax.ShapeDtypeStruct((t, LANES), F32)],
        compiler_params=_cparams(("parallel",)),
        name="odd_in",
    )(x, w, wg, bg)


def _mlstm_kernel(qf_ref, kf_ref, vf_ref, gf_ref, qb_ref, kb_ref, vb_ref, gb_ref, hf_ref, hb_ref, c_scr, m_scr):
    L = CHUNK

    @pl.when(pl.program_id(1) == 0)
    def _():
        c_scr[...] = jnp.zeros_like(c_scr)
        m_scr[...] = jnp.zeros_like(m_scr)

    row_i = lax.broadcasted_iota(jnp.int32, (L, L), 0)
    col_i = lax.broadcasted_iota(jnp.int32, (L, L), 1)
    ones_col = jnp.where(col_i == 0, 1.0, 0.0).astype(BF16)
    nt = (((1,), (1,)), ((), ()))
    tn = (((0,), (0,)), ((), ()))
    refs = ((qf_ref, kf_ref, vf_ref, gf_ref, hf_ref), (qb_ref, kb_ref, vb_ref, gb_ref, hb_ref))
    units = [(d, h) for d in range(2) for h in range(C_HEADS)]
    cols = lambda h: slice(h * LANES, (h + 1) * LANES)

    gate = []
    for d in range(2):
        past = (col_i <= row_i) if d == 0 else (col_i >= row_i)
        gates = refs[d][3][...]
        csum = jnp.dot(past.astype(F32), gates, preferred_element_type=F32, precision=lax.Precision.HIGHEST)
        gate.append((past, gates, csum, gates.T, csum.T))

    qk, qc, c_old, m_old = {}, {}, {}, {}
    for u in units:
        d, h = u
        q_ref, k_ref = refs[d][0], refs[d][1]
        sd = 2 * h + d
        c_old[u] = c_scr[sd]
        m_old[u] = m_scr[sd:sd + 1, 0:1]
        qk[u] = lax.dot_general(q_ref[:, cols(h)], k_ref[:, cols(h)], nt, preferred_element_type=F32)
        qc[u] = jnp.dot(q_ref[:, cols(h)], c_old[u].astype(BF16), preferred_element_type=F32)

    att, kw, w_inter, m_t, decay, m_new = {}, {}, {}, {}, {}, {}
    for u in units:
        d, h = u
        past, gates, csum, gates_t, csum_t = gate[d]
        i_col, f_col = 2 * C_HEADS * d + h, 2 * C_HEADS * d + C_HEADS + h
        last = L - 1 if d == 0 else 0
        b = csum[:, f_col:f_col + 1]
        ic = gates[:, i_col:i_col + 1]
        r_row = gates_t[i_col:i_col + 1, :] - csum_t[f_col:f_col + 1, :]
        dmat = jnp.where(past, b + r_row, -jnp.inf)
        inter = b + m_old[u]
        m_t[u] = jnp.maximum(inter, jnp.max(dmat, axis=-1, keepdims=True))
        w_inter[u] = jnp.exp(inter - m_t[u])
        att[u] = (jnp.exp(dmat - m_t[u]) * qk[u]).astype(BF16)
        b_last = b[last:last + 1, :]
        g_col = b_last - b + ic
        m_new[u] = jnp.maximum(b_last + m_old[u], jnp.max(g_col, axis=0, keepdims=True))
        decay[u] = jnp.exp(b_last + m_old[u] - m_new[u])
        kw[u] = (jnp.exp(g_col - m_new[u]) * refs[d][1][:, cols(h)].astype(F32)).astype(BF16)

    pv, upd = {}, {}
    for u in units:
        d, h = u
        v_ext = jnp.concatenate([refs[d][2][:, cols(h)], ones_col], axis=1)
        pv[u] = jnp.dot(att[u], v_ext, preferred_element_type=F32)
        upd[u] = lax.dot_general(kw[u], v_ext, tn, preferred_element_type=F32)

    for u in units:
        d, h = u
        sd = 2 * h + d
        num = w_inter[u] * qc[u] + pv[u]
        den = num[:, LANES:LANES + 1]
        h_ref = refs[d][4]
        h_ref[:, cols(h)] = (num[:, 0:LANES] / jnp.maximum(jnp.abs(den), jnp.exp(-m_t[u]))).astype(h_ref.dtype)
        c_scr[sd] = decay[u] * c_old[u] + upd[u]
        m_scr[sd:sd + 1, :] = jnp.broadcast_to(m_new[u], (1, LANES))


def _mlstm(q, k, v, g, row0, batch, seq):
    nc = seq // CHUNK
    c0 = row0 // CHUNK
    fwd = lambda b, c: (c0 + b * nc + c, 0)
    bwd = lambda b, c: (c0 + b * nc + nc - 1 - c, 0)
    big = lambda m: pl.BlockSpec((CHUNK, D_MODEL), m)
    small = lambda m: pl.BlockSpec((CHUNK, LANES), m)
    out = jax.ShapeDtypeStruct((batch * seq, D_MODEL), BF16)
    return pl.pallas_call(
        _mlstm_kernel,
        grid=(batch, nc),
        in_specs=[big(fwd), big(fwd), big(fwd), small(fwd), big(bwd), big(bwd), big(bwd), small(bwd)],
        out_specs=[big(lambda b, c: (b * nc + c, 0)), big(lambda b, c: (b * nc + nc - 1 - c, 0))],
        out_shape=[out, out],
        scratch_shapes=[pltpu.VMEM((2 * C_HEADS, CHUNK, 2 * LANES), F32), pltpu.VMEM((2 * C_HEADS, LANES), F32)],
        compiler_params=_cparams(("parallel", "arbitrary")),
        name="mlstm",
    )(q, k, v, g, q, k, v, g)


MOE_BM = 256
DISPATCH_TM = 256


def _dispatch_kernel(zero_ref, dest_ref, x_ref, xs_hbm, zbuf, sem):
    tm = x_ref.shape[0]

    @pl.when(pl.program_id(0) == 0)
    def _():
        zbuf[...] = jnp.zeros_like(zbuf)
        for z in range(2 * N_EXPERTS):
            cp = pltpu.make_async_copy(zbuf, xs_hbm.at[pl.ds(pl.multiple_of(zero_ref[z], MOE_BM), MOE_BM), :], sem)
            cp.start()
            cp.wait()

    def issue(t, c):
        for kk in range(TOP_K):
            d = dest_ref[0, kk, t]
            pltpu.make_async_copy(x_ref.at[pl.ds(t, 1), :], xs_hbm.at[pl.ds(d, 1), :], sem).start()
        return c

    lax.fori_loop(0, tm, issue, 0, unroll=8)
    for _ in range(TOP_K):
        pltpu.make_async_copy(x_ref, xs_hbm.at[pl.ds(0, tm), :], sem).wait()


def _dispatch(x1, dest, zero_rows, n_slots):
    t = x1.shape[0]
    tm = DISPATCH_TM
    return pl.pallas_call(
        _dispatch_kernel,
        grid_spec=pltpu.PrefetchScalarGridSpec(
            num_scalar_prefetch=1,
            grid=(t // tm,),
            in_specs=[pl.BlockSpec((1, TOP_K, tm), lambda i, z: (i, 0, 0), memory_space=pltpu.SMEM),
                      pl.BlockSpec((tm, D_MODEL), lambda i, z: (i, 0))],
            out_specs=pl.BlockSpec(memory_space=pl.ANY),
            scratch_shapes=[pltpu.VMEM((MOE_BM, D_MODEL), F32), pltpu.SemaphoreType.DMA(())],
        ),
        out_shape=jax.ShapeDtypeStruct((n_slots, D_MODEL), F32),
        compiler_params=_cparams(("arbitrary",)),
        name="moe_dispatch",
    )(zero_rows, dest, x1)


def _experts_kernel(be_ref, nv_ref, xs_ref, wg_ref, wu_ref, wd_ref, ys_ref):
    i = pl.program_id(0)
    nv = nv_ref[i]

    @pl.when(nv > 0)
    def _():
        xb = xs_ref[...].astype(BF16)
        hg = jnp.dot(xb, wg_ref[...], preferred_element_type=F32)
        hu = jnp.dot(xb, wu_ref[...], preferred_element_type=F32)
        hh = (jax.nn.silu(hg) * hu).astype(BF16)
        ys_ref[...] = jnp.dot(hh, wd_ref[...], preferred_element_type=F32)

    @pl.when(nv == 0)
    def _():
        ys_ref[...] = jnp.zeros_like(ys_ref)


def _experts(xs, block_e, n_valid, wg, wu, wd):
    n_slots = xs.shape[0]
    bm = MOE_BM
    wspec = lambda s: pl.BlockSpec((None,) + s, lambda i, be, nv: (be[i], 0, 0))
    return pl.pallas_call(
        _experts_kernel,
        grid_spec=pltpu.PrefetchScalarGridSpec(
            num_scalar_prefetch=2,
            grid=(n_slots // bm,),
            in_specs=[pl.BlockSpec((bm, D_MODEL), lambda i, be, nv: (i, 0)),
                      wspec((D_MODEL, D_EXPERT)), wspec((D_MODEL, D_EXPERT)), wspec((D_EXPERT, D_MODEL))],
            out_specs=pl.BlockSpec((bm, D_MODEL), lambda i, be, nv: (i, 0)),
        ),
        out_shape=jax.ShapeDtypeStruct((n_slots, D_MODEL), F32),
        compiler_params=_cparams(("arbitrary",)),
        name="moe_experts",
    )(block_e, n_valid, xs, wg, wu, wd)


def _combine_kernel(dest_ref, ys_hbm, x_ref, gate_ref, lg_ref, lb_ref, o_ref, ybuf, sem):
    tm = x_ref.shape[0]

    def issue(t, c):
        for kk in range(TOP_K):
            d = dest_ref[0, kk, t]
            pltpu.make_async_copy(ys_hbm.at[pl.ds(d, 1), :], ybuf.at[kk, pl.ds(t, 1), :], sem).start()
        return c

    lax.fori_loop(0, tm, issue, 0, unroll=8)
    for kk in range(TOP_K):
        pltpu.make_async_copy(ys_hbm.at[pl.ds(0, tm), :], ybuf.at[kk], sem).wait()
    y = gate_ref[:, 0:1] * ybuf[0] + gate_ref[:, 1:2] * ybuf[1]
    o_ref[...] = _layer_norm(ALPHA * x_ref[...] + y, lg_ref[...], lb_ref[...])


def _combine(ys, dest, x1, gate, lg, lb):
    t = x1.shape[0]
    tm = DISPATCH_TM
    return pl.pallas_call(
        _combine_kernel,
        grid=(t // tm,),
        in_specs=[pl.BlockSpec((1, TOP_K, tm), lambda i: (i, 0, 0), memory_space=pltpu.SMEM),
                  pl.BlockSpec(memory_space=pl.ANY),
                  pl.BlockSpec((tm, D_MODEL), lambda i: (i, 0)),
                  pl.BlockSpec((tm, TOP_K), lambda i: (i, 0)),
                  _full(lg.shape), _full(lb.shape)],
        out_specs=pl.BlockSpec((tm, D_MODEL), lambda i: (i, 0)),
        out_shape=jax.ShapeDtypeStruct((t, D_MODEL), F32),
        scratch_shapes=[pltpu.VMEM((TOP_K, tm, D_MODEL), F32), pltpu.SemaphoreType.DMA(())],
        compiler_params=_cparams(("arbitrary",)),
        name="moe_combine",
    )(dest,ys, x1, gate, lg, lb)


def _slots(ri, cnt):
    t = ri.shape[1]
    bm = MOE_BM
    counts = cnt[:, 0].astype(jnp.int32)
    padded = (counts + bm - 1) // bm * bm
    pad_end = jnp.cumsum(padded)
    pad_start = pad_end - padded
    experts, rank = ri[0:TOP_K], ri[TOP_K:2 * TOP_K]
    start = jnp.sum(jnp.where(experts[:, :, None] == jnp.arange(N_EXPERTS)[None, None, :], pad_start[None, None, :], 0),
                    axis=-1)
    dest = (start + rank).astype(jnp.int32)
    n_blocks = (t * TOP_K) // bm + N_EXPERTS
    blk0 = jnp.arange(n_blocks, dtype=jnp.int32) * bm
    ge = blk0[:, None] >= pad_end[None, :]
    block_e = jnp.minimum(jnp.sum(ge, axis=1), N_EXPERTS - 1).astype(jnp.int32)
    seg_end = jnp.sum(jnp.where(block_e[:, None] == jnp.arange(N_EXPERTS)[None, :], (pad_start + counts)[None, :], 0),
                      axis=1)
    n_valid = jnp.where(blk0 < pad_end[-1], jnp.clip(seg_end - blk0, 0, bm), 0).astype(jnp.int32)
    tm = DISPATCH_TM
    dest = dest.reshape(TOP_K, t // tm, tm).transpose(1, 0, 2)
    tail = jnp.minimum(pad_end[-1] + jnp.arange(N_EXPERTS, dtype=jnp.int32) * bm, (n_blocks - 1) * bm)
    zero_rows = jnp.concatenate([jnp.maximum(pad_end - bm, 0), tail]).astype(jnp.int32)
    return dest, block_e, n_valid, zero_rows, n_blocks * bm


def _moe(x1, ri, rf, cnt, wg, wu, wd, lg, lb):
    dest, block_e, n_valid, zero_rows, n_slots = _slots(ri, cnt)
    xs = _dispatch(x1, dest, zero_rows, n_slots)
    ys = _experts(xs, block_e, n_valid, wg, wu, wd)
    return _combine(ys, dest, x1, rf[0:TOP_K].T, lg, lb)


def _rope_table(seq):
    inv = jnp.power(ROPE_THETA, -jnp.arange(0, QK_ROPE, 2, dtype=F32) / QK_ROPE)
    ang = jnp.arange(seq, dtype=F32)[:, None] * inv[None, :]
    cos, sin = jnp.cos(ang), jnp.sin(ang)
    one = jnp.ones((seq, QK_NOPE), F32)
    z64 = jnp.zeros((seq, QK_NOPE), F32)
    z32 = jnp.zeros((seq, LANES - QK_NOPE - QK_ROPE), F32)
    return jnp.concatenate([one, cos, cos, z32, z64, sin, sin, z32], axis=1)


def _swap_pairs(w):
    half = QK_ROPE // 2
    return jnp.concatenate([-w[..., half:], w[..., :half]], axis=-1)


def _even_weights(w_in, gm_g, gm_b, w_sp, b_sp, q_g, w_uq, kv_g, w_ukv):
    o0, o1, o2, o3 = GM_WIDTH, 2 * GM_WIDTH, 2 * GM_WIDTH + Q_LORA, 2 * GM_WIDTH + Q_LORA + KV_LORA
    w_kr = w_in[:, o3:]
    z64 = jnp.zeros((D_MODEL, QK_NOPE), F32)
    z32 = jnp.zeros((D_MODEL, LANES - QK_NOPE - QK_ROPE), F32)
    wa = jnp.concatenate([w_in[:, :o3], z64, w_kr, z32, z64, _swap_pairs(w_kr), z32], axis=1).astype(BF16)
    wq = w_uq.reshape(Q_LORA, MLA_HEADS, QK_NOPE + QK_ROPE)
    nope, rope = wq[..., :QK_NOPE], wq[..., QK_NOPE:]
    zq32 = jnp.zeros((Q_LORA, MLA_HEADS, LANES - QK_NOPE - QK_ROPE), F32)
    zq64 = jnp.zeros((Q_LORA, MLA_HEADS, QK_NOPE), F32)
    q1 = jnp.concatenate([nope, rope, zq32], axis=-1).reshape(Q_LORA, D_MODEL)
    q2 = jnp.concatenate([zq64, _swap_pairs(rope), zq32], axis=-1).reshape(Q_LORA, D_MODEL)
    wq = jnp.concatenate([q1, q2], axis=1).astype(BF16)
    wkv = w_ukv.reshape(KV_LORA, MLA_HEADS, QK_NOPE + V_HEAD)
    kn = jnp.concatenate([wkv[..., :QK_NOPE], jnp.zeros((KV_LORA, MLA_HEADS, LANES - QK_NOPE), F32)], axis=-1)
    wkv = jnp.concatenate([kn.reshape(KV_LORA, D_MODEL), wkv[..., QK_NOPE:].reshape(KV_LORA, MLA_HEADS * V_HEAD)],
                          axis=1).astype(BF16)
    bsp = jnp.broadcast_to(b_sp[:, :, None], (GM_HEADS, CHUNK, LANES)).astype(F32)
    return (wa, gm_g.reshape(1, GM_WIDTH), gm_b.reshape(1, GM_WIDTH), w_sp.astype(BF16), bsp,
            q_g.reshape(1, Q_LORA), wq, kv_g.reshape(1, KV_LORA), wkv)


def _router_weights(router_w):
    whi = router_w.astype(BF16)
    wlo = (router_w - whi.astype(F32)).astype(BF16)
    pad = jnp.zeros((D_MODEL, LANES - 2 * N_EXPERTS), BF16)
    top = jnp.concatenate([whi, wlo, pad], axis=1)
    bot = jnp.concatenate([whi, jnp.zeros_like(wlo), pad], axis=1)
    return jnp.concatenate([top, bot], axis=0)


def kernel(x_prompt, x_sample, even_w_in, gm_norm_g, gm_norm_b, gm_w_spatial, gm_b_spatial, mla_q_norm_g, mla_w_uq, mla_kv_norm_g, mla_w_ukv, even_w_out, mlstm_w_in, mlstm_b_gates, mlstm_norm_g, mlstm_norm_b, mlstm_w_out, router_w, router_bias, moe_w_gate, moe_w_up, moe_w_down, ln_mix_g, ln_mix_b, ln_ffn_g, ln_ffn_b):
    bp, sp, _ = x_prompt.shape
    bs, ss, _ = x_sample.shape
    n_prompt = bp * sp
    x = jnp.concatenate([x_prompt.reshape(n_prompt, D_MODEL), x_sample.reshape(bs * ss, D_MODEL)], axis=0)
    groups = ((0, bp, sp), (n_prompt, bs, ss))
    tab = _rope_table(max(sp, ss))
    row = lambda a: a.reshape(1, -1).astype(F32)
    strict_upper = (jnp.arange(OUT_TM)[:, None] < jnp.arange(OUT_TM)[None, :]).astype(BF16)
    route_consts = (_router_weights(router_w), jnp.broadcast_to(router_bias.astype(F32)[:, None], (N_EXPERTS, LANES)),
                    strict_upper)

    for l in range(DEPTH):
        j = l // 2
        if l % 2 == 0:
            wts = _even_weights(even_w_in[j], gm_norm_g[j], gm_norm_b[j], gm_w_spatial[j], gm_b_spatial[j],
                                mla_q_norm_g[j], mla_w_uq[j], mla_kv_norm_g[j], mla_w_ukv[j])
            a, q, k, vt = _even_in(x, tab, wts, n_prompt, sp, ss)
            o_p, o_s = [_attention(q, k, vt, r0, b, s) for r0, b, s in groups]
            consts = (even_w_out[j].astype(BF16), row(ln_mix_g[l]), row(ln_mix_b[l])) + route_consts
            acts = ((a, None), (o_p, 0), (o_s, 1))
            x1, ri, rf, cnt = _mixer_out(_even_out_kernel, acts, x, consts, n_prompt)
        else:
            w_in = mlstm_w_in[j]
            wg = jnp.concatenate([w_in[:, 4 * D_MODEL:], jnp.zeros((D_MODEL, LANES - 4 * C_HEADS), F32)], axis=1)
            bg = jnp.concatenate([mlstm_b_gates[j], jnp.zeros((LANES - 4 * C_HEADS,), F32)]).reshape(1, LANES)
            q, k, v, og, g = _odd_in(x, w_in[:, :4 * D_MODEL].astype(BF16), wg.astype(BF16), bg)
            (hf_p, hb_p), (hf_s, hb_s) = [_mlstm(q, k, v, g, r0, b, s) for r0, b, s in groups]
            consts = (row(mlstm_norm_g[j]), row(mlstm_norm_b[j]), mlstm_w_out[j].astype(BF16),
                      row(ln_mix_g[l]), row(ln_mix_b[l])) + route_consts
            acts = ((hf_p, 0), (hb_p, 0), (hf_s, 1), (hb_s, 1), (og, None))
            x1, ri, rf, cnt = _mixer_out(_odd_out_kernel, acts, x, consts, n_prompt)
        x = _moe(x1, ri, rf, cnt, moe_w_gate[l].astype(BF16), moe_w_up[l].astype(BF16),
                 moe_w_down[l].astype(BF16), row(ln_ffn_g[l]), row(ln_ffn_b[l]))

    return x[:n_prompt].reshape(bp, sp, D_MODEL), x[n_prompt:].reshape(bs, ss, D_MODEL)
```

```python
import functools

import jax
import jax.numpy as jnp
from jax import lax
from jax.experimental import pallas as pl
from jax.experimental.pallas import tpu as pltpu

F32 = jnp.float32
BF16 = jnp.bfloat16

D_MODEL = 1024
DEPTH = 4
CHUNK = 128
GM_WIDTH = 512
GM_HEADS = 4
MLA_HEADS = 8
QK_NOPE = 64
QK_ROPE = 32
V_HEAD = 64
Q_LORA = 384
KV_LORA = 256
ROPE_THETA = 10000.0
C_HEADS = 8
C_HEAD_DIM = 128
N_EXPERTS = 16
N_GROUPS = 4
EXPERTS_PER_GROUP = 4
TOP_K = 2
D_EXPERT = 512
ALPHA = (2 * DEPTH) ** 0.25
EPS = 1e-5
LOG2_E = 1.4426950408889634
LANES = 128

VMEM_LIMIT = 56 * 1024 * 1024


def _cparams(sem):
    return pltpu.CompilerParams(dimension_semantics=sem, vmem_limit_bytes=VMEM_LIMIT)


def _layer_norm(y, g, b):
    mu = jnp.mean(y, axis=-1, keepdims=True)
    yc = y - mu
    var = jnp.mean(yc * yc, axis=-1, keepdims=True)
    return yc * lax.rsqrt(var + EPS) * g + b


def _rms_norm(y, g):
    return y * lax.rsqrt(jnp.mean(y * y, axis=-1, keepdims=True) + EPS) * g


def _full(shape):
    return pl.BlockSpec(shape, lambda *_: (0,) * len(shape))


EVEN_TM = 512


def _even_in_kernel(x_ref, tab_ref, wa_ref, gmg_ref, gmb_ref, wsp_ref, bsp_ref, qg_ref, wq_ref, kvg_ref, wkv_ref,
                    a_ref, q_ref, k_ref, vt_ref):
    tm = x_ref.shape[0]
    z = jnp.dot(x_ref[...].astype(BF16), wa_ref[...], preferred_element_type=F32)
    ct = tab_ref[:, 0:LANES]
    st = tab_ref[:, LANES:2 * LANES]

    for g in range(GM_HEADS):
        lo = GM_WIDTH + g * LANES
        gv = jax.nn.gelu(z[:, lo:lo + LANES], approximate=True)
        gv = _layer_norm(gv, gmg_ref[:, g * LANES:(g + 1) * LANES], gmb_ref[:, g * LANES:(g + 1) * LANES]).astype(BF16)
        gu = jax.nn.gelu(z[:, g * LANES:(g + 1) * LANES], approximate=True)
        for c in range(tm // CHUNK):
            rows = slice(c * CHUNK, (c + 1) * CHUNK)
            sv = jnp.dot(wsp_ref[g], gv[rows], preferred_element_type=F32) + bsp_ref[g]
            a_ref[rows, g * LANES:(g + 1) * LANES] = (gu[rows] * sv).astype(a_ref.dtype)

    cq = _rms_norm(z[:, 2 * GM_WIDTH:2 * GM_WIDTH + Q_LORA], qg_ref[...]).astype(BF16)
    qq = jnp.dot(cq, wq_ref[...], preferred_element_type=F32)
    scale = (QK_NOPE + QK_ROPE) ** -0.5 * LOG2_E
    for h in range(MLA_HEADS):
        q1 = qq[:, h * LANES:(h + 1) * LANES]
        q2 = qq[:, D_MODEL + h * LANES:D_MODEL + (h + 1) * LANES]
        q_ref[:, h * LANES:(h + 1) * LANES] = ((q1 * ct + q2 * st) * scale).astype(q_ref.dtype)

    base = 2 * GM_WIDTH + Q_LORA
    ckv = _rms_norm(z[:, base:base + KV_LORA], kvg_ref[...]).astype(BF16)
    kv = jnp.dot(ckv, wkv_ref[...], preferred_element_type=F32)
    base += KV_LORA
    kr = z[:, base:base + LANES] * ct + z[:, base + LANES:base + 2 * LANES] * st
    for h in range(MLA_HEADS):
        k_ref[:, h * LANES:(h + 1) * LANES] = (kv[:, h * LANES:(h + 1) * LANES] + kr).astype(k_ref.dtype)
    vt_ref[...] = kv[:, D_MODEL:].T.astype(vt_ref.dtype)


def _even_in(x, tab, wts, n_prompt_rows, s_prompt, s_sample):
    t = x.shape[0]
    tm = EVEN_TM
    wa, gmg, gmb, wsp, bsp, qg, wq, kvg, wkv = wts

    def tab_map(i):
        return (jnp.where(i < n_prompt_rows // tm, i % (s_prompt // tm), i % (s_sample // tm)), 0)

    row = lambda w: pl.BlockSpec((tm, w), lambda i: (i, 0))
    return pl.pallas_call(
        _even_in_kernel,
        grid=(t // tm,),
        in_specs=[row(D_MODEL), pl.BlockSpec((tm, 2 * LANES), tab_map), _full(wa.shape), _full(gmg.shape),
                  _full(gmb.shape), _full(wsp.shape), _full(bsp.shape), _full(qg.shape), _full(wq.shape),
                  _full(kvg.shape), _full(wkv.shape)],
        out_specs=[row(GM_WIDTH), row(D_MODEL), row(D_MODEL),
                   pl.BlockSpec((MLA_HEADS * V_HEAD, tm), lambda i: (0, i))],
        out_shape=[jax.ShapeDtypeStruct((t, GM_WIDTH), BF16), jax.ShapeDtypeStruct((t, D_MODEL), BF16),
                   jax.ShapeDtypeStruct((t, D_MODEL), BF16), jax.ShapeDtypeStruct((MLA_HEADS * V_HEAD, t), BF16)],
        compiler_params=_cparams(("parallel",)),
        name="even_in",
    )(x, tab, wa, gmg, gmb, wsp, bsp, qg, wq, kvg, wkv)


ATT_TQ = 512
ATT_TK = 256
ATT_SUM_ROWS = 16
ATT_UNROLL = 8


def _attn_kernel(q_ref, k_ref, vt_ref, o_ref, sa0, sa1, sb0, sb1, *, seq, tk):
    tq = q_ref.shape[0]
    n = seq // tk
    qs = [q_ref[:, j * LANES:(j + 1) * LANES] for j in range(2)]
    slots = ((sa0, sa1), (sb0, sb1))

    def scores(j, c):
        r0 = pl.multiple_of(jnp.minimum(c, n - 1) * tk, tk)
        kc = k_ref[pl.ds(r0, tk), j * LANES:(j + 1) * LANES]
        return lax.dot_general(kc, qs[j], (((1,), (1,)), ((), ())), preferred_element_type=F32)

    def consume(c, cur, nxt, carry):
        for j in range(2):
            nxt[j][...] = scores(j, c + 1)
        r0 = pl.multiple_of(c * tk, tk)
        new = []
        for j in range(2):
            m, acc = carry[j]
            st = cur[j][...]
            m_new = jnp.maximum(m, jnp.max(st, axis=0, keepdims=True))
            p = jnp.exp2((st - m_new).astype(BF16))
            a = jnp.exp2(m - m_new)
            vtc = jnp.concatenate([vt_ref[j * V_HEAD:(j + 1) * V_HEAD, pl.ds(r0, tk)], ones_rows], axis=0)
            acc = a * acc + jnp.dot(vtc, p, preferred_element_type=F32)
            new.append((m_new, acc))
        return tuple(new)

    def step(i, carry):
        for u in range(ATT_UNROLL):
            carry = consume(ATT_UNROLL * i + u, slots[u % 2], slots[(u + 1) % 2], carry)
        return carry

    ones_rows = jnp.ones((ATT_SUM_ROWS, tk), BF16)
    for j in range(2):
        slots[0][j][...] = scores(j, 0)
    init = tuple((jnp.full((1, tq), -jnp.inf, F32), jnp.zeros((V_HEAD + ATT_SUM_ROWS, tq), F32)) for _ in range(2))
    res = lax.fori_loop(0, n // ATT_UNROLL, step, init)
    ot = jnp.concatenate([acc[0:V_HEAD] / acc[V_HEAD:V_HEAD + 1] for _, acc in res], axis=0)
    o_ref[...] = ot.T.astype(o_ref.dtype)


def _attention(q, k, vt, row0, batch, seq):
    tq = ATT_TQ
    nq = seq // tq
    assert seq % (ATT_TK * ATT_UNROLL) == 0 and seq % tq == 0 and row0 % seq == 0
    return pl.pallas_call(
        functools.partial(_attn_kernel, seq=seq, tk=ATT_TK),
        grid=(batch, MLA_HEADS // 2, nq),
        in_specs=[pl.BlockSpec((tq, 2 * LANES), lambda b, h, i: (row0 // tq + b * nq + i, h)),
                  pl.BlockSpec((seq, 2 * LANES), lambda b, h, i: (row0 // seq + b, h)),
                  pl.BlockSpec((2 * V_HEAD, seq), lambda b, h, i: (h, row0 // seq + b))],
        out_specs=pl.BlockSpec((tq, LANES), lambda b, h, i: (b * nq + i, h)),
        out_shape=jax.ShapeDtypeStruct((batch * seq, MLA_HEADS * V_HEAD), BF16),
        scratch_shapes=[pltpu.VMEM((ATT_TK, tq), F32)] * 4,
        compiler_params=_cparams(("parallel", "parallel", "parallel")),
        name="attention",
    )(q, k, vt)


OUT_TM = 512
ROUTE_ROWS = 8


def _first_of(vals, target):
    idx = jnp.full(target.shape, len(vals) - 1, jnp.int32)
    for j in range(len(vals) - 2, -1, -1):
        idx = jnp.where(vals[j] == target, j, idx)
    return idx


def _pick(vals, idx):
    out = vals[-1]
    for j in range(len(vals) - 2, -1, -1):
        out = jnp.where(idx == j, vals[j], out)
    return out


def _route_rows(logits, bias_col):
    scores = jax.nn.sigmoid(logits)
    biased = scores + bias_col
    b = [biased[e:e + 1, :] for e in range(N_EXPERTS)]
    s = [scores[e:e + 1, :] for e in range(N_EXPERTS)]
    gs = []
    for g in range(N_GROUPS):
        v0, v1, v2, v3 = b[EXPERTS_PER_GROUP * g:EXPERTS_PER_GROUP * (g + 1)]
        hi1, lo1 = jnp.maximum(v0, v1), jnp.minimum(v0, v1)
        hi2, lo2 = jnp.maximum(v2, v3), jnp.minimum(v2, v3)
        gs.append(jnp.maximum(hi1, hi2) + jnp.maximum(jnp.minimum(hi1, hi2), jnp.maximum(lo1, lo2)))
    g_sel = _first_of(gs, functools.reduce(jnp.maximum, gs))
    v = [_pick([b[EXPERTS_PER_GROUP * g + j] for g in range(N_GROUPS)], g_sel) for j in range(EXPERTS_PER_GROUP)]
    sc = [_pick([s[EXPERTS_PER_GROUP * g + j] for g in range(N_GROUPS)], g_sel) for j in range(EXPERTS_PER_GROUP)]
    i1 = _first_of(v, functools.reduce(jnp.maximum, v))
    rest = [jnp.where(i1 == j, -jnp.inf, v[j]) for j in range(EXPERTS_PER_GROUP)]
    i2 = _first_of(rest, functools.reduce(jnp.maximum, rest))
    s1, s2 = _pick(sc, i1), _pick(sc, i2)
    den = s1 + s2
    return g_sel * EXPERTS_PER_GROUP + i1, g_sel * EXPERTS_PER_GROUP + i2, s1 / den, s2 / den


def _post_mix(mix, x_ref, lg_ref, lb_ref, wr_ref, rb_ref, su_ref, x1_ref, ri_ref, rf_ref, cnt_ref, carry_scr):
    @pl.when(pl.program_id(0) == 0)
    def _():
        carry_scr[...] = jnp.zeros_like(carry_scr)

    tm = x_ref.shape[0]
    x1 = _layer_norm(ALPHA * x_ref[...] + mix, lg_ref[...], lb_ref[...])
    x1_ref[...] = x1
    hi = x1.astype(BF16)
    lo = (x1 - hi.astype(F32)).astype(BF16)
    lgt = jnp.dot(jnp.concatenate([hi, lo], axis=1), wr_ref[...], preferred_element_type=F32).T
    e1, e2, g1, g2 = _route_rows(lgt[0:N_EXPERTS] + lgt[N_EXPERTS:2 * N_EXPERTS], rb_ref[:, 0:1])

    erow = lax.broadcasted_iota(jnp.int32, (N_EXPERTS, tm), 0)
    o1, o2 = erow == e1, erow == e2
    oh = jnp.where(o1 | o2, 1.0, 0.0)
    before = jnp.dot(oh.astype(BF16), su_ref[...], preferred_element_type=F32) + carry_scr[:, 0:1]
    r1 = jnp.sum(jnp.where(o1, before, 0.0), axis=0, keepdims=True).astype(jnp.int32)
    r2 = jnp.sum(jnp.where(o2, before, 0.0), axis=0, keepdims=True).astype(jnp.int32)
    carry_scr[...] = carry_scr[...] + jnp.sum(oh, axis=1, keepdims=True)
    cnt_ref[...] = carry_scr[...]
    ri_ref[...] = jnp.concatenate([e1, e2, r1, r2, jnp.zeros((ROUTE_ROWS - 4, tm), jnp.int32)], axis=0)
    rf_ref[...] = jnp.concatenate([g1, g2, jnp.zeros((ROUTE_ROWS - 2, tm), F32)], axis=0)


def _even_out_kernel(a_ref, op_ref, os_ref, x_ref, w_ref, *rest, prompt_tiles):
    o = jnp.where(pl.program_id(0) < prompt_tiles, op_ref[...], os_ref[...])
    mix = jnp.dot(a_ref[...], w_ref[0:GM_WIDTH, :], preferred_element_type=F32)
    mix = mix + jnp.dot(o, w_ref[GM_WIDTH:, :], preferred_element_type=F32)
    _post_mix(mix, x_ref, *rest)


def _odd_out_kernel(hfp_ref, hbp_ref, hfs_ref, hbs_ref, og_ref, x_ref, ng_ref, nb_ref, w_ref, *rest, prompt_tiles):
    is_prompt = pl.program_id(0) < prompt_tiles
    parts = []
    for h in range(C_HEADS):
        cols = slice(h * LANES, (h + 1) * LANES)
        hf = jnp.where(is_prompt, hfp_ref[:, cols], hfs_ref[:, cols])
        hb = jnp.where(is_prompt, hbp_ref[:, cols], hbs_ref[:, cols])
        hs = hf.astype(F32) + hb.astype(F32)
        parts.append((_layer_norm(hs, ng_ref[:, cols], nb_ref[:, cols]) * og_ref[:, cols].astype(F32)).astype(BF16))
    mix = jnp.dot(jnp.concatenate(parts, axis=1), w_ref[...], preferred_element_type=F32)
    _post_mix(mix, x_ref, *rest)


def _mixer_out(body, acts, x, consts, n_prompt_rows):
    t = x.shape[0]
    tm = OUT_TM
    pt = n_prompt_rows // tm
    maps = {None: lambda i: (i, 0), 0: lambda i: (jnp.minimum(i, pt - 1), 0), 1: lambda i: (jnp.maximum(i - pt, 0), 0)}
    row = lambda w: pl.BlockSpec((tm, w), lambda i: (i, 0))
    route_spec = pl.BlockSpec((ROUTE_ROWS, tm), lambda i: (0, i))
    return pl.pallas_call(
        functools.partial(body, prompt_tiles=pt),
        grid=(t // tm,),
        in_specs=[pl.BlockSpec((tm, a.shape[1]), maps[g]) for a, g in acts] + [row(D_MODEL)]
        + [_full(c.shape) for c in consts],
        out_specs=[row(D_MODEL), route_spec, route_spec, _full((N_EXPERTS, LANES))],
        out_shape=[jax.ShapeDtypeStruct((t, D_MODEL), F32), jax.ShapeDtypeStruct((ROUTE_ROWS, t), jnp.int32),
                   jax.ShapeDtypeStruct((ROUTE_ROWS, t), F32), jax.ShapeDtypeStruct((N_EXPERTS, LANES), F32)],
        scratch_shapes=[pltpu.VMEM((N_EXPERTS, LANES), F32)],
        compiler_params=_cparams(("arbitrary",)),
        name=body.__name__.strip("_"),
    )(*[a for a, _ in acts], x, *consts)


ODD_TM = 512


def _odd_in_kernel(x_ref, w_ref, wg_ref, bg_ref, q_ref, k_ref, kt_ref, vt_ref, og_ref, g_ref):
    xb = x_ref[...].astype(BF16)
    q_ref[...] = jnp.dot(xb, w_ref[:, 0:D_MODEL], preferred_element_type=F32).astype(q_ref.dtype)
    k = jnp.dot(xb, w_ref[:, D_MODEL:2 * D_MODEL], preferred_element_type=F32) * (C_HEAD_DIM ** -0.5)
    k_ref[...] = k.astype(k_ref.dtype)
    kt_ref[...] = k.T.astype(kt_ref.dtype)
    v = jnp.dot(xb, w_ref[:, 2 * D_MODEL:3 * D_MODEL], preferred_element_type=F32)
    vt_ref[...] = v.T.astype(vt_ref.dtype)
    o = jnp.dot(xb, w_ref[:, 3 * D_MODEL:4 * D_MODEL], preferred_element_type=F32)
    og_ref[...] = jax.nn.sigmoid(o).astype(og_ref.dtype)
    g = jnp.dot(xb, wg_ref[...], preferred_element_type=F32) + bg_ref[...]
    lane = lax.broadcasted_iota(jnp.int32, g.shape, 1)
    is_f = ((lane >= C_HEADS) & (lane < 2 * C_HEADS)) | ((lane >= 3 * C_HEADS) & (lane < 4 * C_HEADS))
    g_ref[...] = jnp.where(is_f, jax.nn.log_sigmoid(g), g)


def _odd_in(x, w, wg, bg):
    t = x.shape[0]
    tm = ODD_TM
    row = lambda w_: pl.BlockSpec((tm, w_), lambda i: (i, 0))
    col = pl.BlockSpec((D_MODEL, tm), lambda i: (0, i))
    big = jax.ShapeDtypeStruct((t, D_MODEL), BF16)
    big_t = jax.ShapeDtypeStruct((D_MODEL, t), BF16)
    return pl.pallas_call(
        _odd_in_kernel,
        grid=(t // tm,),
        in_specs=[row(D_MODEL), _full(w.shape), _full(wg.shape), _full(bg.shape)],
        out_specs=[row(D_MODEL), row(D_MODEL), col, col, row(D_MODEL), row(LANES)],
        out_shape=[big, big, big_t, big_t, big, jax.ShapeDtypeStruct((t, LANES), F32)],
        compiler_params=_cparams(("parallel",)),
        name="odd_in",
    )(x, w, wg, bg)


def _mlstm_kernel(qf_ref, kf_ref, ktf_ref, vtf_ref, gf_ref, qb_ref, kb_ref, ktb_ref, vtb_ref, gb_ref,
                  hf_ref, hb_ref, c_scr, m_scr):
    L = CHUNK

    @pl.when(pl.program_id(1) == 0)
    def _():
        c_scr[...] = jnp.zeros_like(c_scr)
        m_scr[...] = jnp.zeros_like(m_scr)

    row_i = lax.broadcasted_iota(jnp.int32, (L, L), 0)
    col_i = lax.broadcasted_iota(jnp.int32, (L, L), 1)
    ones_row = jnp.where(row_i == 0, 1.0, 0.0).astype(BF16)
    nt = (((1,), (1,)), ((), ()))
    refs = ((qf_ref, kf_ref, ktf_ref, vtf_ref, gf_ref, hf_ref), (qb_ref, kb_ref, ktb_ref, vtb_ref, gb_ref, hb_ref))
    units = [(d, h) for d in range(2) for h in range(C_HEADS)]
    cols = lambda h: slice(h * LANES, (h + 1) * LANES)

    gate = []
    for d in range(2):
        past = (col_i <= row_i) if d == 0 else (col_i >= row_i)
        vis = (row_i <= col_i) if d == 0 else (row_i >= col_i)
        gates = refs[d][4][...]
        csum = jnp.dot(past.astype(F32), gates, preferred_element_type=F32, precision=lax.Precision.HIGHEST)
        gate.append((vis, gates, csum, gates.T, csum.T))

    qk_t, cq, c_old, m_old = {}, {}, {}, {}
    for u in units:
        d, h = u
        q = refs[d][0][:, cols(h)]
        sd = 2 * h + d
        c_old[u] = c_scr[sd]
        m_old[u] = m_scr[sd:sd + 1, :]
        qk_t[u] = lax.dot_general(refs[d][1][:, cols(h)], q, nt, preferred_element_type=F32)
        cq[u] = lax.dot_general(c_old[u].astype(BF16), q, nt, preferred_element_type=F32)

    att_t, kw_t, w_row, m_row, decay, m_new = {}, {}, {}, {}, {}, {}
    for u in units:
        d, h = u
        vis, gates, csum, gates_t, csum_t = gate[d]
        i_col, f_col = 2 * C_HEADS * d + h, 2 * C_HEADS * d + C_HEADS + h
        last = L - 1 if d == 0 else 0
        r_col = gates[:, i_col:i_col + 1] - csum[:, f_col:f_col + 1]
        b_row = csum_t[f_col:f_col + 1, :]
        ic_row = gates_t[i_col:i_col + 1, :]
        dmat_t = jnp.where(vis, r_col + b_row, -jnp.inf)
        inter = b_row + m_old[u]
        m_row[u] = jnp.maximum(inter, jnp.max(dmat_t, axis=0, keepdims=True))
        w_row[u] = jnp.exp(inter - m_row[u])
        att_t[u] = (jnp.exp(dmat_t - m_row[u]) * qk_t[u]).astype(BF16)
        b_last = jnp.broadcast_to(b_row[:, last:last + 1], (1, L))
        g_row = b_last - b_row + ic_row
        m_new[u] = jnp.maximum(b_last + m_old[u], jnp.max(g_row, axis=1, keepdims=True))
        decay[u] = jnp.exp(b_last + m_old[u] - m_new[u])
        kw_t[u] = (refs[d][2][cols(h), :].astype(F32) * jnp.exp(g_row - m_new[u])).astype(BF16)

    pv_t, upd = {}, {}
    for u in units:
        d, h = u
        vt_ext = jnp.concatenate([refs[d][3][cols(h), :], ones_row], axis=0)
        pv_t[u] = jnp.dot(vt_ext, att_t[u], preferred_element_type=F32)
        upd[u] = lax.dot_general(vt_ext, kw_t[u], nt, preferred_element_type=F32)

    for u in units:
        d, h = u
        sd = 2 * h + d
        num_t = w_row[u] * cq[u] + pv_t[u]
        den = num_t[LANES:LANES + 1, :]
        h_t = num_t[0:LANES, :] / jnp.maximum(jnp.abs(den), jnp.exp(-m_row[u]))
        h_ref = refs[d][5]
        h_ref[:, cols(h)] = h_t.T.astype(h_ref.dtype)
        c_scr[sd] = decay[u] * c_old[u] + upd[u]
        m_scr[sd:sd + 1, :] = m_new[u]


def _mlstm(q, k, kt, vt, g, row0, batch, seq):
    nc = seq // CHUNK
    c0 = row0 // CHUNK
    fwd = lambda b, c: c0 + b * nc + c
    bwd = lambda b, c: c0 + b * nc + nc - 1 - c
    big = lambda m: pl.BlockSpec((CHUNK, D_MODEL), lambda b, c: (m(b, c), 0))
    big_t = lambda m: pl.BlockSpec((D_MODEL, CHUNK), lambda b, c: (0, m(b, c)))
    small = lambda m: pl.BlockSpec((CHUNK, LANES), lambda b, c: (m(b, c), 0))
    out = jax.ShapeDtypeStruct((batch * seq, D_MODEL), BF16)
    return pl.pallas_call(
        _mlstm_kernel,
        grid=(batch, nc),
        in_specs=[big(fwd), big(fwd), big_t(fwd), big_t(fwd), small(fwd),
                  big(bwd), big(bwd), big_t(bwd), big_t(bwd), small(bwd)],
        out_specs=[pl.BlockSpec((CHUNK, D_MODEL), lambda b, c: (b * nc + c, 0)),
                   pl.BlockSpec((CHUNK, D_MODEL), lambda b, c: (b * nc + nc - 1 - c, 0))],
        out_shape=[out, out],
        scratch_shapes=[pltpu.VMEM((2 * C_HEADS, 2 * LANES, CHUNK), F32), pltpu.VMEM((2 * C_HEADS, LANES), F32)],
        compiler_params=_cparams(("parallel", "arbitrary")),
        name="mlstm",
    )(q, k, kt, vt, g, q, k, kt, vt, g)


MOE_BM = 512
DISPATCH_TM = 256


def _dispatch_kernel(zero_ref, dest_ref, x_ref, xs_hbm, zbuf, sem):
    tm = x_ref.shape[0]

    @pl.when(pl.program_id(0) == 0)
    def _():
        zbuf[...] = jnp.zeros_like(zbuf)
        for z in range(2 * N_EXPERTS):
            cp = pltpu.make_async_copy(zbuf, xs_hbm.at[pl.ds(pl.multiple_of(zero_ref[z], MOE_BM), MOE_BM), :], sem)
            cp.start()
            cp.wait()

    def issue(t, c):
        for kk in range(TOP_K):
            d = dest_ref[0, kk, t]
            pltpu.make_async_copy(x_ref.at[pl.ds(t, 1), :], xs_hbm.at[pl.ds(d, 1), :], sem).start()
        return c

    lax.fori_loop(0, tm, issue, 0, unroll=8)
    for _ in range(TOP_K):
        pltpu.make_async_copy(x_ref, xs_hbm.at[pl.ds(0, tm), :], sem).wait()


def _dispatch(x1, dest, zero_rows, n_slots):
    t = x1.shape[0]
    tm = DISPATCH_TM
    return pl.pallas_call(
        _dispatch_kernel,
        grid_spec=pltpu.PrefetchScalarGridSpec(
            num_scalar_prefetch=1,
            grid=(t // tm,),
            in_specs=[pl.BlockSpec((1, TOP_K, tm), lambda i, z: (i, 0, 0), memory_space=pltpu.SMEM),
                      pl.BlockSpec((tm, D_MODEL), lambda i, z: (i, 0))],
            out_specs=pl.BlockSpec(memory_space=pl.ANY),
            scratch_shapes=[pltpu.VMEM((MOE_BM, D_MODEL), F32), pltpu.SemaphoreType.DMA(())],
        ),
        out_shape=jax.ShapeDtypeStruct((n_slots, D_MODEL), F32),
        compiler_params=_cparams(("arbitrary",)),
        name="moe_dispatch",
    )(zero_rows, dest, x1)


def _experts_kernel(be_ref, nv_ref, xs_ref, wg_ref, wu_ref, wd_ref, ys_ref):
    i = pl.program_id(0)
    nv = nv_ref[i]

    @pl.when(nv > 0)
    def _():
        xb = xs_ref[...].astype(BF16)
        hg = jnp.dot(xb, wg_ref[...], preferred_element_type=F32)
        hu = jnp.dot(xb, wu_ref[...], preferred_element_type=F32)
        hh = (jax.nn.silu(hg) * hu).astype(BF16)
        ys_ref[...] = jnp.dot(hh, wd_ref[...], preferred_element_type=F32)

    @pl.when(nv == 0)
    def _():
        ys_ref[...] = jnp.zeros_like(ys_ref)


def _experts(xs, block_e, n_valid, wg, wu, wd):
    n_slots = xs.shape[0]
    bm = MOE_BM
    wspec = lambda s: pl.BlockSpec((None,) + s, lambda i, be, nv: (be[i], 0, 0))
    return pl.pallas_call(
        _experts_kernel,
        grid_spec=pltpu.PrefetchScalarGridSpec(
            num_scalar_prefetch=2,
            grid=(n_slots // bm,),
            in_specs=[pl.BlockSpec((bm, D_MODEL), lambda i, be, nv: (i, 0)),
                      wspec((D_MODEL, D_EXPERT)), wspec((D_MODEL, D_EXPERT)), wspec((D_EXPERT, D_MODEL))],
            out_specs=pl.BlockSpec((bm, D_MODEL), lambda i, be, nv: (i, 0)),
        ),
        out_shape=jax.ShapeDtypeStruct((n_slots, D_MODEL), F32),
        compiler_params=_cparams(("arbitrary",)),
        name="moe_experts",
    )(block_e, n_valid, xs, wg, wu, wd)


def _combine_kernel(dest_ref, next_ref, ys_hbm, x_ref, gate_ref, lg_ref, lb_ref, o_ref, ybuf, sems):
    tm = x_ref.shape[0]
    i = pl.program_id(0)
    n = pl.num_programs(0)
    slot = i % 2

    def gather(idx_ref, s):
        def issue(t, c):
            for kk in range(TOP_K):
                d = idx_ref[0, kk, t]
                pltpu.make_async_copy(ys_hbm.at[pl.ds(d, 1), :], ybuf.at[s, kk, pl.ds(t, 1), :], sems.at[s]).start()
            return c

        lax.fori_loop(0, tm, issue, 0, unroll=8)

    @pl.when(i == 0)
    def _():
        gather(dest_ref, 0)

    @pl.when(i + 1 < n)
    def _():
        gather(next_ref, 1 - slot)

    for kk in range(TOP_K):
        pltpu.make_async_copy(ys_hbm.at[pl.ds(0, tm), :], ybuf.at[slot, kk], sems.at[slot]).wait()
    y = gate_ref[:, 0:1] * ybuf[slot, 0] + gate_ref[:, 1:2] * ybuf[slot, 1]
    o_ref[...] = _layer_norm(ALPHA * x_ref[...] + y, lg_ref[...], lb_ref[...])


def _combine(ys, dest, x1, gate, lg, lb, tile0, n_tiles):
    tm = DISPATCH_TM
    last = tile0 + n_tiles - 1
    return pl.pallas_call(
        _combine_kernel,
        grid=(n_tiles,),
        in_specs=[pl.BlockSpec((1, TOP_K, tm), lambda i: (tile0 + i, 0, 0), memory_space=pltpu.SMEM),
                  pl.BlockSpec((1, TOP_K, tm), lambda i: (jnp.minimum(tile0 + i + 1, last), 0, 0),
                               memory_space=pltpu.SMEM),
                  pl.BlockSpec(memory_space=pl.ANY),
                  pl.BlockSpec((tm, D_MODEL), lambda i: (tile0 + i, 0)),
                  pl.BlockSpec((tm, TOP_K), lambda i: (tile0 + i, 0)),
                  _full(lg.shape), _full(lb.shape)],
        out_specs=pl.BlockSpec((tm, D_MODEL), lambda i: (i, 0)),
        out_shape=jax.ShapeDtypeStruct((n_tiles * tm, D_MODEL), F32),
        scratch_shapes=[pltpu.VMEM((2, TOP_K, tm, D_MODEL), F32), pltpu.SemaphoreType.DMA((2,))],
        compiler_params=_cparams(("arbitrary",)),
        name="moe_combine",
    )(dest, dest, ys, x1, gate, lg, lb)


def _slots(ri, cnt):
    t = ri.shape[1]
    bm = MOE_BM
    counts = cnt[:, 0].astype(jnp.int32)
    padded = (counts + bm - 1) // bm * bm
    pad_end = jnp.cumsum(padded)
    pad_start = pad_end - padded
    experts, rank = ri[0:TOP_K], ri[TOP_K:2 * TOP_K]
    start = jnp.sum(jnp.where(experts[:, :, None] == jnp.arange(N_EXPERTS)[None, None, :], pad_start[None, None, :], 0),
                    axis=-1)
    dest = (start + rank).astype(jnp.int32)
    n_blocks = (t * TOP_K) // bm + N_EXPERTS
    blk0 = jnp.arange(n_blocks, dtype=jnp.int32) * bm
    ge = blk0[:, None] >= pad_end[None, :]
    block_e = jnp.minimum(jnp.sum(ge, axis=1), N_EXPERTS - 1).astype(jnp.int32)
    seg_end = jnp.sum(jnp.where(block_e[:, None] == jnp.arange(N_EXPERTS)[None, :], (pad_start + counts)[None, :], 0),
                      axis=1)
    n_valid = jnp.where(blk0 < pad_end[-1], jnp.clip(seg_end - blk0, 0, bm), 0).astype(jnp.int32)
    tm = DISPATCH_TM
    dest = dest.reshape(TOP_K, t // tm, tm).transpose(1, 0, 2)
    tail = jnp.minimum(pad_end[-1] + jnp.arange(N_EXPERTS, dtype=jnp.int32) * bm, (n_blocks - 1) * bm)
    zero_rows = jnp.concatenate([jnp.maximum(pad_end - bm, 0), tail]).astype(jnp.int32)
    return dest, block_e, n_valid, zero_rows, n_blocks * bm


def _moe(x1, ri, rf, cnt, wg, wu, wd, lg, lb, splits):
    dest, block_e, n_valid, zero_rows, n_slots = _slots(ri, cnt)
    xs = _dispatch(x1, dest, zero_rows, n_slots)
    ys = _experts(xs, block_e, n_valid, wg, wu, wd)
    gate = rf[0:TOP_K].T
    return [_combine(ys, dest, x1, gate, lg, lb, t0, nt) for t0, nt in splits]


def _rope_table(seq):
    inv = jnp.power(ROPE_THETA, -jnp.arange(0, QK_ROPE, 2, dtype=F32) / QK_ROPE)
    ang = jnp.arange(seq, dtype=F32)[:, None] * inv[None, :]
    cos, sin = jnp.cos(ang), jnp.sin(ang)
    one = jnp.ones((seq, QK_NOPE), F32)
    z64 = jnp.zeros((seq, QK_NOPE), F32)
    z32 = jnp.zeros((seq, LANES - QK_NOPE - QK_ROPE), F32)
    return jnp.concatenate([one, cos, cos, z32, z64, sin, sin, z32], axis=1)


def _swap_pairs(w):
    half = QK_ROPE // 2
    return jnp.concatenate([-w[..., half:], w[..., :half]], axis=-1)


def _even_weights(w_in, gm_g, gm_b, w_sp, b_sp, q_g, w_uq, kv_g, w_ukv):
    o0, o1, o2, o3 = GM_WIDTH, 2 * GM_WIDTH, 2 * GM_WIDTH + Q_LORA, 2 * GM_WIDTH + Q_LORA + KV_LORA
    w_kr = w_in[:, o3:]
    z64 = jnp.zeros((D_MODEL, QK_NOPE), F32)
    z32 = jnp.zeros((D_MODEL, LANES - QK_NOPE - QK_ROPE), F32)
    wa = jnp.concatenate([w_in[:, :o3], z64, w_kr, z32, z64, _swap_pairs(w_kr), z32], axis=1).astype(BF16)
    wq = w_uq.reshape(Q_LORA, MLA_HEADS, QK_NOPE + QK_ROPE)
    nope, rope = wq[..., :QK_NOPE], wq[..., QK_NOPE:]
    zq32 = jnp.zeros((Q_LORA, MLA_HEADS, LANES - QK_NOPE - QK_ROPE), F32)
    zq64 = jnp.zeros((Q_LORA, MLA_HEADS, QK_NOPE), F32)
    q1 = jnp.concatenate([nope, rope, zq32], axis=-1).reshape(Q_LORA, D_MODEL)
    q2 = jnp.concatenate([zq64, _swap_pairs(rope), zq32], axis=-1).reshape(Q_LORA, D_MODEL)
    wq = jnp.concatenate([q1, q2], axis=1).astype(BF16)
    wkv = w_ukv.reshape(KV_LORA, MLA_HEADS, QK_NOPE + V_HEAD)
    kn = jnp.concatenate([wkv[..., :QK_NOPE], jnp.zeros((KV_LORA, MLA_HEADS, LANES - QK_NOPE), F32)], axis=-1)
    wkv = jnp.concatenate([kn.reshape(KV_LORA, D_MODEL), wkv[..., QK_NOPE:].reshape(KV_LORA, MLA_HEADS * V_HEAD)],
                          axis=1).astype(BF16)
    bsp = jnp.broadcast_to(b_sp[:, :, None], (GM_HEADS, CHUNK, LANES)).astype(F32)
    return (wa, gm_g.reshape(1, GM_WIDTH), gm_b.reshape(1, GM_WIDTH), w_sp.astype(BF16), bsp,
            q_g.reshape(1, Q_LORA), wq, kv_g.reshape(1, KV_LORA), wkv)


def _router_weights(router_w):
    whi = router_w.astype(BF16)
    wlo = (router_w - whi.astype(F32)).astype(BF16)
    pad = jnp.zeros((D_MODEL, LANES - 2 * N_EXPERTS), BF16)
    top = jnp.concatenate([whi, wlo, pad], axis=1)
    bot = jnp.concatenate([whi, jnp.zeros_like(wlo), pad], axis=1)
    return jnp.concatenate([top, bot], axis=0)


def kernel(x_prompt, x_sample, even_w_in, gm_norm_g, gm_norm_b, gm_w_spatial, gm_b_spatial, mla_q_norm_g, mla_w_uq, mla_kv_norm_g, mla_w_ukv, even_w_out, mlstm_w_in, mlstm_b_gates, mlstm_norm_g, mlstm_norm_b, mlstm_w_out, router_w, router_bias, moe_w_gate, moe_w_up, moe_w_down, ln_mix_g, ln_mix_b, ln_ffn_g, ln_ffn_b):
    bp, sp, _ = x_prompt.shape
    bs, ss, _ = x_sample.shape
    n_prompt = bp * sp
    x = jnp.concatenate([x_prompt.reshape(n_prompt, D_MODEL), x_sample.reshape(bs * ss, D_MODEL)], axis=0)
    groups = ((0, bp, sp), (n_prompt, bs, ss))
    tab = _rope_table(max(sp, ss))
    row = lambda a: a.reshape(1, -1).astype(F32)
    strict_upper = (jnp.arange(OUT_TM)[:, None] < jnp.arange(OUT_TM)[None, :]).astype(BF16)
    route_consts = (_router_weights(router_w), jnp.broadcast_to(router_bias.astype(F32)[:, None], (N_EXPERTS, LANES)),
                    strict_upper)

    for l in range(DEPTH):
        j = l // 2
        if l % 2 == 0:
            wts = _even_weights(even_w_in[j], gm_norm_g[j], gm_norm_b[j], gm_w_spatial[j], gm_b_spatial[j],
                                mla_q_norm_g[j], mla_w_uq[j], mla_kv_norm_g[j], mla_w_ukv[j])
            a, q, k, vt = _even_in(x, tab, wts, n_prompt, sp, ss)
            o_p, o_s = [_attention(q, k, vt, r0, b, s) for r0, b, s in groups]
            consts = (even_w_out[j].astype(BF16), row(ln_mix_g[l]), row(ln_mix_b[l])) + route_consts
            acts = ((a, None), (o_p, 0), (o_s, 1))
            x1, ri, rf, cnt = _mixer_out(_even_out_kernel, acts, x, consts, n_prompt)
        else:
            w_in = mlstm_w_in[j]
            wg = jnp.concatenate([w_in[:, 4 * D_MODEL:], jnp.zeros((D_MODEL, LANES - 4 * C_HEADS), F32)], axis=1)
            bg = jnp.concatenate([mlstm_b_gates[j], jnp.zeros((LANES - 4 * C_HEADS,), F32)]).reshape(1, LANES)
            q, k, kt, vt, og, g = _odd_in(x, w_in[:, :4 * D_MODEL].astype(BF16), wg.astype(BF16), bg)
            (hf_p, hb_p), (hf_s, hb_s) = [_mlstm(q, k, kt, vt, g, r0, b, s) for r0, b, s in groups]
            consts = (row(mlstm_norm_g[j]), row(mlstm_norm_b[j]), mlstm_w_out[j].astype(BF16),
                      row(ln_mix_g[l]), row(ln_mix_b[l])) + route_consts
            acts = ((hf_p, 0), (hb_p, 0), (hf_s, 1), (hb_s, 1), (og, None))
            x1, ri, rf, cnt = _mixer_out(_odd_out_kernel, acts, x, consts, n_prompt)
        p_tiles, all_tiles = n_prompt // DISPATCH_TM, x.shape[0] // DISPATCH_TM
        splits = ((0, all_tiles),) if l + 1 < DEPTH else ((0, p_tiles), (p_tiles, all_tiles - p_tiles))
        outs = _moe(x1, ri, rf, cnt, moe_w_gate[l].astype(BF16), moe_w_up[l].astype(BF16),
                    moe_w_down[l].astype(BF16), row(ln_ffn_g[l]), row(ln_ffn_b[l]), splits)
        x = outs[0]

    y_prompt, y_sample = outs
    return y_prompt.reshape(bp, sp, D_MODEL), y_sample.reshape(bs, ss, D_MODEL)
```

```python
import functools

import jax
import jax.numpy as jnp
from jax import lax
from jax.experimental import pallas as pl
from jax.experimental.pallas import tpu as pltpu

F32 = jnp.float32
BF16 = jnp.bfloat16

D_MODEL = 1024
DEPTH = 4
CHUNK = 128
GM_WIDTH = 512
GM_HEADS = 4
MLA_HEADS = 8
QK_NOPE = 64
QK_ROPE = 32
V_HEAD = 64
Q_LORA = 384
KV_LORA = 256
ROPE_THETA = 10000.0
C_HEADS = 8
C_HEAD_DIM = 128
N_EXPERTS = 16
N_GROUPS = 4
EXPERTS_PER_GROUP = 4
TOP_K = 2
D_EXPERT = 512
ALPHA = (2 * DEPTH) ** 0.25
EPS = 1e-5
LOG2_E = 1.4426950408889634
LANES = 128

VMEM_LIMIT = 56 * 1024 * 1024


def _cparams(sem):
    return pltpu.CompilerParams(dimension_semantics=sem, vmem_limit_bytes=VMEM_LIMIT)


def _layer_norm(y, g, b):
    mu = jnp.mean(y, axis=-1, keepdims=True)
    yc = y - mu
    var = jnp.mean(yc * yc, axis=-1, keepdims=True)
    return yc * lax.rsqrt(var + EPS) * g + b


def _rms_norm(y, g):
    return y * lax.rsqrt(jnp.mean(y * y, axis=-1, keepdims=True) + EPS) * g


def _full(shape):
    return pl.BlockSpec(shape, lambda *_: (0,) * len(shape))


EVEN_TM = 512


def _even_in_kernel(x_ref, tab_ref, wa_ref, gmg_ref, gmb_ref, wsp_ref, bsp_ref, qg_ref, wq_ref, kvg_ref, wkv_ref,
                    a_ref, q_ref, k_ref, vt_ref):
    tm = x_ref.shape[0]
    z = jnp.dot(x_ref[...].astype(BF16), wa_ref[...], preferred_element_type=F32)
    ct = tab_ref[:, 0:LANES]
    st = tab_ref[:, LANES:2 * LANES]

    for g in range(GM_HEADS):
        lo = GM_WIDTH + g * LANES
        gv = jax.nn.gelu(z[:, lo:lo + LANES], approximate=True)
        gv = _layer_norm(gv, gmg_ref[:, g * LANES:(g + 1) * LANES], gmb_ref[:, g * LANES:(g + 1) * LANES]).astype(BF16)
        gu = jax.nn.gelu(z[:, g * LANES:(g + 1) * LANES], approximate=True)
        for c in range(tm // CHUNK):
            rows = slice(c * CHUNK, (c + 1) * CHUNK)
            sv = jnp.dot(wsp_ref[g], gv[rows], preferred_element_type=F32) + bsp_ref[g]
            a_ref[rows, g * LANES:(g + 1) * LANES] = (gu[rows] * sv).astype(a_ref.dtype)

    cq = _rms_norm(z[:, 2 * GM_WIDTH:2 * GM_WIDTH + Q_LORA], qg_ref[...]).astype(BF16)
    qq = jnp.dot(cq, wq_ref[...], preferred_element_type=F32)
    scale = (QK_NOPE + QK_ROPE) ** -0.5 * LOG2_E
    for h in range(MLA_HEADS):
        q1 = qq[:, h * LANES:(h + 1) * LANES]
        q2 = qq[:, D_MODEL + h * LANES:D_MODEL + (h + 1) * LANES]
        q_ref[:, h * LANES:(h + 1) * LANES] = ((q1 * ct + q2 * st) * scale).astype(q_ref.dtype)

    base = 2 * GM_WIDTH + Q_LORA
    ckv = _rms_norm(z[:, base:base + KV_LORA], kvg_ref[...]).astype(BF16)
    kv = jnp.dot(ckv, wkv_ref[...], preferred_element_type=F32)
    base += KV_LORA
    kr = z[:, base:base + LANES] * ct + z[:, base + LANES:base + 2 * LANES] * st
    for h in range(MLA_HEADS):
        k_ref[:, h * LANES:(h + 1) * LANES] = (kv[:, h * LANES:(h + 1) * LANES] + kr).astype(k_ref.dtype)
    vt_ref[...] = kv[:, D_MODEL:].T.astype(vt_ref.dtype)


def _even_in(x, tab, wts, n_prompt_rows, s_prompt, s_sample):
    t = x.shape[0]
    tm = EVEN_TM
    wa, gmg, gmb, wsp, bsp, qg, wq, kvg, wkv = wts

    def tab_map(i):
        return (jnp.where(i < n_prompt_rows // tm, i % (s_prompt // tm), i % (s_sample // tm)), 0)

    row = lambda w: pl.BlockSpec((tm, w), lambda i: (i, 0))
    return pl.pallas_call(
        _even_in_kernel,
        grid=(t // tm,),
        in_specs=[row(D_MODEL), pl.BlockSpec((tm, 2 * LANES), tab_map), _full(wa.shape), _full(gmg.shape),
                  _full(gmb.shape), _full(wsp.shape), _full(bsp.shape), _full(qg.shape), _full(wq.shape),
                  _full(kvg.shape), _full(wkv.shape)],
        out_specs=[row(GM_WIDTH), row(D_MODEL), row(D_MODEL),
                   pl.BlockSpec((MLA_HEADS * V_HEAD, tm), lambda i: (0, i))],
        out_shape=[jax.ShapeDtypeStruct((t, GM_WIDTH), BF16), jax.ShapeDtypeStruct((t, D_MODEL), BF16),
                   jax.ShapeDtypeStruct((t, D_MODEL), BF16), jax.ShapeDtypeStruct((MLA_HEADS * V_HEAD, t), BF16)],
        compiler_params=_cparams(("parallel",)),
        name="even_in",
    )(x, tab, wa, gmg, gmb, wsp, bsp, qg, wq, kvg, wkv)


ATT_TQ = 512
ATT_TK = 256
ATT_SUM_ROWS = 16
ATT_UNROLL = 8


def _attn_kernel(q_ref, k_ref, vt_ref, o_ref, sa0, sa1, sb0, sb1, *, seq, tk):
    tq = q_ref.shape[0]
    n = seq // tk
    qs = [q_ref[:, j * LANES:(j + 1) * LANES] for j in range(2)]
    slots = ((sa0, sa1), (sb0, sb1))

    def scores(j, c):
        r0 = pl.multiple_of(jnp.minimum(c, n - 1) * tk, tk)
        kc = k_ref[pl.ds(r0, tk), j * LANES:(j + 1) * LANES]
        return lax.dot_general(kc, qs[j], (((1,), (1,)), ((), ())), preferred_element_type=F32)

    def consume(c, cur, nxt, carry):
        for j in range(2):
            nxt[j][...] = scores(j, c + 1)
        r0 = pl.multiple_of(c * tk, tk)
        new = []
        for j in range(2):
            m, acc = carry[j]
            st = cur[j][...]
            m_new = jnp.maximum(m, jnp.max(st, axis=0, keepdims=True))
            p = jnp.exp2((st - m_new).astype(BF16))
            a = jnp.exp2(m - m_new)
            vtc = jnp.concatenate([vt_ref[j * V_HEAD:(j + 1) * V_HEAD, pl.ds(r0, tk)], ones_rows], axis=0)
            acc = a * acc + jnp.dot(vtc, p, preferred_element_type=F32)
            new.append((m_new, acc))
        return tuple(new)

    def step(i, carry):
        for u in range(ATT_UNROLL):
            carry = consume(ATT_UNROLL * i + u, slots[u % 2], slots[(u + 1) % 2], carry)
        return carry

    ones_rows = jnp.ones((ATT_SUM_ROWS, tk), BF16)
    for j in range(2):
        slots[0][j][...] = scores(j, 0)
    init = tuple((jnp.full((1, tq), -jnp.inf, F32), jnp.zeros((V_HEAD + ATT_SUM_ROWS, tq), F32)) for _ in range(2))
    res = lax.fori_loop(0, n // ATT_UNROLL, step, init)
    ot = jnp.concatenate([acc[0:V_HEAD] / acc[V_HEAD:V_HEAD + 1] for _, acc in res], axis=0)
    o_ref[...] = ot.T.astype(o_ref.dtype)


def _attention(q, k, vt, row0, batch, seq):
    tq = ATT_TQ
    nq = seq // tq
    assert seq % (ATT_TK * ATT_UNROLL) == 0 and seq % tq == 0 and row0 % seq == 0
    return pl.pallas_call(
        functools.partial(_attn_kernel, seq=seq, tk=ATT_TK),
        grid=(batch, MLA_HEADS // 2, nq),
        in_specs=[pl.BlockSpec((tq, 2 * LANES), lambda b, h, i: (row0 // tq + b * nq + i, h)),
                  pl.BlockSpec((seq, 2 * LANES), lambda b, h, i: (row0 // seq + b, h)),
                  pl.BlockSpec((2 * V_HEAD, seq), lambda b, h, i: (h, row0 // seq + b))],
        out_specs=pl.BlockSpec((tq, LANES), lambda b, h, i: (b * nq + i, h)),
        out_shape=jax.ShapeDtypeStruct((batch * seq, MLA_HEADS * V_HEAD), BF16),
        scratch_shapes=[pltpu.VMEM((ATT_TK, tq), F32)] * 4,
        compiler_params=_cparams(("parallel", "parallel", "parallel")),
        name="attention",
    )(q, k, vt)


OUT_TM = 512
OUT_SPLIT = 2
ROUTE_ROWS = 8


def _first_of(vals, target):
    idx = jnp.full(target.shape, len(vals) - 1, jnp.int32)
    for j in range(len(vals) - 2, -1, -1):
        idx = jnp.where(vals[j] == target, j, idx)
    return idx


def _pick(vals, idx):
    out = vals[-1]
    for j in range(len(vals) - 2, -1, -1):
        out = jnp.where(idx == j, vals[j], out)
    return out


def _route_rows(logits, bias_col):
    scores = jax.nn.sigmoid(logits)
    biased = scores + bias_col
    b = [biased[e:e + 1, :] for e in range(N_EXPERTS)]
    s = [scores[e:e + 1, :] for e in range(N_EXPERTS)]
    gs = []
    for g in range(N_GROUPS):
        v0, v1, v2, v3 = b[EXPERTS_PER_GROUP * g:EXPERTS_PER_GROUP * (g + 1)]
        hi1, lo1 = jnp.maximum(v0, v1), jnp.minimum(v0, v1)
        hi2, lo2 = jnp.maximum(v2, v3), jnp.minimum(v2, v3)
        gs.append(jnp.maximum(hi1, hi2) + jnp.maximum(jnp.minimum(hi1, hi2), jnp.maximum(lo1, lo2)))
    g_sel = _first_of(gs, functools.reduce(jnp.maximum, gs))
    v = [_pick([b[EXPERTS_PER_GROUP * g + j] for g in range(N_GROUPS)], g_sel) for j in range(EXPERTS_PER_GROUP)]
    sc = [_pick([s[EXPERTS_PER_GROUP * g + j] for g in range(N_GROUPS)], g_sel) for j in range(EXPERTS_PER_GROUP)]
    i1 = _first_of(v, functools.reduce(jnp.maximum, v))
    rest = [jnp.where(i1 == j, -jnp.inf, v[j]) for j in range(EXPERTS_PER_GROUP)]
    i2 = _first_of(rest, functools.reduce(jnp.maximum, rest))
    s1, s2 = _pick(sc, i1), _pick(sc, i2)
    den = s1 + s2
    return g_sel * EXPERTS_PER_GROUP + i1, g_sel * EXPERTS_PER_GROUP + i2, s1 / den, s2 / den


def _post_mix(mix_of, x_ref, lg_ref, lb_ref, wr_ref, rb_ref, su_ref, x1_ref, ri_ref, rf_ref, cnt_ref, carry_scr):
    @pl.when(pl.program_id(0) == 0)
    def _():
        carry_scr[...] = jnp.zeros_like(carry_scr)

    tm = x_ref.shape[0]
    lgts = []
    for r in range(OUT_SPLIT):
        rows = slice(r * tm // OUT_SPLIT, (r + 1) * tm // OUT_SPLIT)
        x1 = _layer_norm(ALPHA * x_ref[rows, :] + mix_of(rows), lg_ref[...], lb_ref[...])
        x1_ref[rows, :] = x1
        hi = x1.astype(BF16)
        lo = (x1 - hi.astype(F32)).astype(BF16)
        lg = jnp.dot(hi, wr_ref[0:D_MODEL, :], preferred_element_type=F32)
        lg = lg + jnp.dot(lo, wr_ref[D_MODEL:, :], preferred_element_type=F32)
        lgts.append(lg.T)
    lgt = jnp.concatenate(lgts, axis=1)
    e1, e2, g1, g2 = _route_rows(lgt[0:N_EXPERTS] + lgt[N_EXPERTS:2 * N_EXPERTS], rb_ref[:, 0:1])

    erow = lax.broadcasted_iota(jnp.int32, (N_EXPERTS, tm), 0)
    o1, o2 = erow == e1, erow == e2
    oh = jnp.where(o1 | o2, 1.0, 0.0)
    before = jnp.dot(oh.astype(BF16), su_ref[...], preferred_element_type=F32) + carry_scr[:, 0:1]
    r1 = jnp.sum(jnp.where(o1, before, 0.0), axis=0, keepdims=True).astype(jnp.int32)
    r2 = jnp.sum(jnp.where(o2, before, 0.0), axis=0, keepdims=True).astype(jnp.int32)
    carry_scr[...] = carry_scr[...] + jnp.sum(oh, axis=1, keepdims=True)
    cnt_ref[...] = carry_scr[...]
    ri_ref[...] = jnp.concatenate([e1, e2, r1, r2, jnp.zeros((ROUTE_ROWS - 4, tm), jnp.int32)], axis=0)
    rf_ref[...] = jnp.concatenate([g1, g2, jnp.zeros((ROUTE_ROWS - 2, tm), F32)], axis=0)


def _even_out_kernel(a_ref, op_ref, os_ref, x_ref, w_ref, *rest, prompt_tiles):
    is_prompt = pl.program_id(0) < prompt_tiles

    def mix_of(rows):
        o = jnp.where(is_prompt, op_ref[rows, :], os_ref[rows, :])
        mix = jnp.dot(a_ref[rows, :], w_ref[0:GM_WIDTH, :], preferred_element_type=F32)
        return mix + jnp.dot(o, w_ref[GM_WIDTH:, :], preferred_element_type=F32)

    _post_mix(mix_of, x_ref, *rest)


def _odd_out_kernel(hfp_ref, hbp_ref, hfs_ref, hbs_ref, og_ref, x_ref, ng_ref, nb_ref, w_ref, *rest, prompt_tiles):
    is_prompt = pl.program_id(0) < prompt_tiles

    def mix_of(rows):
        parts = []
        for h in range(C_HEADS):
            cols = slice(h * LANES, (h + 1) * LANES)
            hf = jnp.where(is_prompt, hfp_ref[rows, cols], hfs_ref[rows, cols])
            hb = jnp.where(is_prompt, hbp_ref[rows, cols], hbs_ref[rows, cols])
            hs = _layer_norm(hf.astype(F32) + hb.astype(F32), ng_ref[:, cols], nb_ref[:, cols])
            parts.append((hs * og_ref[rows, cols].astype(F32)).astype(BF16))
        return jnp.dot(jnp.concatenate(parts, axis=1), w_ref[...], preferred_element_type=F32)

    _post_mix(mix_of, x_ref, *rest)


def _mixer_out(body, acts, x, consts, n_prompt_rows):
    t = x.shape[0]
    tm = OUT_TM
    pt = n_prompt_rows // tm
    maps = {None: lambda i: (i, 0), 0: lambda i: (jnp.minimum(i, pt - 1), 0), 1: lambda i: (jnp.maximum(i - pt, 0), 0)}
    row = lambda w: pl.BlockSpec((tm, w), lambda i: (i, 0))
    route_spec = pl.BlockSpec((ROUTE_ROWS, tm), lambda i: (0, i))
    return pl.pallas_call(
        functools.partial(body, prompt_tiles=pt),
        grid=(t // tm,),
        in_specs=[pl.BlockSpec((tm, a.shape[1]), maps[g]) for a, g in acts] + [row(D_MODEL)]
        + [_full(c.shape) for c in consts],
        out_specs=[row(D_MODEL), route_spec, route_spec, _full((N_EXPERTS, LANES))],
        out_shape=[jax.ShapeDtypeStruct((t, D_MODEL), F32), jax.ShapeDtypeStruct((ROUTE_ROWS, t), jnp.int32),
                   jax.ShapeDtypeStruct((ROUTE_ROWS, t), F32), jax.ShapeDtypeStruct((N_EXPERTS, LANES), F32)],
        scratch_shapes=[pltpu.VMEM((N_EXPERTS, LANES), F32)],
        compiler_params=_cparams(("arbitrary",)),
        name=body.__name__.strip("_"),
    )(*[a for a, _ in acts], x, *consts)


ODD_TM = 512


def _odd_in_kernel(x_ref, w_ref, wg_ref, bg_ref, q_ref, k_ref, kt_ref, vt_ref, og_ref, g_ref):
    xb = x_ref[...].astype(BF16)
    q_ref[...] = jnp.dot(xb, w_ref[:, 0:D_MODEL], preferred_element_type=F32).astype(q_ref.dtype)
    k = jnp.dot(xb, w_ref[:, D_MODEL:2 * D_MODEL], preferred_element_type=F32) * (C_HEAD_DIM ** -0.5)
    k_ref[...] = k.astype(k_ref.dtype)
    kt_ref[...] = k.T.astype(kt_ref.dtype)
    v = jnp.dot(xb, w_ref[:, 2 * D_MODEL:3 * D_MODEL], preferred_element_type=F32)
    vt_ref[...] = v.T.astype(vt_ref.dtype)
    o = jnp.dot(xb, w_ref[:, 3 * D_MODEL:4 * D_MODEL], preferred_element_type=F32)
    og_ref[...] = jax.nn.sigmoid(o).astype(og_ref.dtype)
    g = jnp.dot(xb, wg_ref[...], preferred_element_type=F32) + bg_ref[...]
    lane = lax.broadcasted_iota(jnp.int32, g.shape, 1)
    is_f = ((lane >= C_HEADS) & (lane < 2 * C_HEADS)) | ((lane >= 3 * C_HEADS) & (lane < 4 * C_HEADS))
    g_ref[...] = jnp.where(is_f, jax.nn.log_sigmoid(g), g)


def _odd_in(x, w, wg, bg):
    t = x.shape[0]
    tm = ODD_TM
    row = lambda w_: pl.BlockSpec((tm, w_), lambda i: (i, 0))
    col = pl.BlockSpec((D_MODEL, tm), lambda i: (0, i))
    big = jax.ShapeDtypeStruct((t, D_MODEL), BF16)
    big_t = jax.ShapeDtypeStruct((D_MODEL, t), BF16)
    return pl.pallas_call(
        _odd_in_kernel,
        grid=(t // tm,),
        in_specs=[row(D_MODEL), _full(w.shape), _full(wg.shape), _full(bg.shape)],
        out_specs=[row(D_MODEL), row(D_MODEL), col, col, row(D_MODEL), row(LANES)],
        out_shape=[big, big, big_t, big_t, big, jax.ShapeDtypeStruct((t, LANES), F32)],
        compiler_params=_cparams(("parallel",)),
        name="odd_in",
    )(x, w, wg, bg)


def _mlstm_kernel(qf_ref, kf_ref, ktf_ref, vtf_ref, gf_ref, qb_ref, kb_ref, ktb_ref, vtb_ref, gb_ref,
                  hf_ref, hb_ref, c_scr, m_scr):
    L = CHUNK

    @pl.when(pl.program_id(1) == 0)
    def _():
        c_scr[...] = jnp.zeros_like(c_scr)
        m_scr[...] = jnp.zeros_like(m_scr)

    row_i = lax.broadcasted_iota(jnp.int32, (L, L), 0)
    col_i = lax.broadcasted_iota(jnp.int32, (L, L), 1)
    ones_row = jnp.where(row_i == 0, 1.0, 0.0).astype(BF16)
    nt = (((1,), (1,)), ((), ()))
    refs = ((qf_ref, kf_ref, ktf_ref, vtf_ref, gf_ref, hf_ref), (qb_ref, kb_ref, ktb_ref, vtb_ref, gb_ref, hb_ref))
    units = [(d, h) for d in range(2) for h in range(C_HEADS)]
    cols = lambda h: slice(h * LANES, (h + 1) * LANES)

    gate = []
    for d in range(2):
        past = (col_i <= row_i) if d == 0 else (col_i >= row_i)
        vis = (row_i <= col_i) if d == 0 else (row_i >= col_i)
        gates = refs[d][4][...]
        csum = jnp.dot(past.astype(F32), gates, preferred_element_type=F32, precision=lax.Precision.HIGHEST)
        gate.append((vis, gates, csum, gates.T, csum.T))

    qk_t, cq, c_old, m_old = {}, {}, {}, {}
    for u in units:
        d, h = u
        q = refs[d][0][:, cols(h)]
        sd = 2 * h + d
        c_old[u] = c_scr[sd]
        m_old[u] = m_scr[sd:sd + 1, :]
        qk_t[u] = lax.dot_general(refs[d][1][:, cols(h)], q, nt, preferred_element_type=F32)
        cq[u] = lax.dot_general(c_old[u].astype(BF16), q, nt, preferred_element_type=F32)

    att_t, kw_t, w_row, m_row, decay, m_new = {}, {}, {}, {}, {}, {}
    for u in units:
        d, h = u
        vis, gates, csum, gates_t, csum_t = gate[d]
        i_col, f_col = 2 * C_HEADS * d + h, 2 * C_HEADS * d + C_HEADS + h
        last = L - 1 if d == 0 else 0
        r_col = gates[:, i_col:i_col + 1] - csum[:, f_col:f_col + 1]
        b_row = csum_t[f_col:f_col + 1, :]
        ic_row = gates_t[i_col:i_col + 1, :]
        dmat_t = jnp.where(vis, r_col + b_row, -jnp.inf)
        inter = b_row + m_old[u]
        m_row[u] = jnp.maximum(inter, jnp.max(dmat_t, axis=0, keepdims=True))
        w_row[u] = jnp.exp(inter - m_row[u])
        att_t[u] = (jnp.exp(dmat_t - m_row[u]) * qk_t[u]).astype(BF16)
        b_last = jnp.broadcast_to(b_row[:, last:last + 1], (1, L))
        g_row = b_last - b_row + ic_row
        m_new[u] = jnp.maximum(b_last + m_old[u], jnp.max(g_row, axis=1, keepdims=True))
        decay[u] = jnp.exp(b_last + m_old[u] - m_new[u])
        kw_t[u] = (refs[d][2][cols(h), :].astype(F32) * jnp.exp(g_row - m_new[u])).astype(BF16)

    pv_t, upd = {}, {}
    for u in units:
        d, h = u
        vt_ext = jnp.concatenate([refs[d][3][cols(h), :], ones_row], axis=0)
        pv_t[u] = jnp.dot(vt_ext, att_t[u], preferred_element_type=F32)
        upd[u] = lax.dot_general(vt_ext, kw_t[u], nt, preferred_element_type=F32)

    for u in units:
        d, h = u
        sd = 2 * h + d
        num_t = w_row[u] * cq[u] + pv_t[u]
        den = num_t[LANES:LANES + 1, :]
        h_t = num_t[0:LANES, :] / jnp.maximum(jnp.abs(den), jnp.exp(-m_row[u]))
        h_ref = refs[d][5]
        h_ref[:, cols(h)] = h_t.T.astype(h_ref.dtype)
        c_scr[sd] = decay[u] * c_old[u] + upd[u]
        m_scr[sd:sd + 1, :] = m_new[u]


def _mlstm(q, k, kt, vt, g, row0, batch, seq):
    nc = seq // CHUNK
    c0 = row0 // CHUNK
    fwd = lambda b, c: c0 + b * nc + c
    bwd = lambda b, c: c0 + b * nc + nc - 1 - c
    big = lambda m: pl.BlockSpec((CHUNK, D_MODEL), lambda b, c: (m(b, c), 0))
    big_t = lambda m: pl.BlockSpec((D_MODEL, CHUNK), lambda b, c: (0, m(b, c)))
    small = lambda m: pl.BlockSpec((CHUNK, LANES), lambda b, c: (m(b, c), 0))
    out = jax.ShapeDtypeStruct((batch * seq, D_MODEL), BF16)
    return pl.pallas_call(
        _mlstm_kernel,
        grid=(batch, nc),
        in_specs=[big(fwd), big(fwd), big_t(fwd), big_t(fwd), small(fwd),
                  big(bwd), big(bwd), big_t(bwd), big_t(bwd), small(bwd)],
        out_specs=[pl.BlockSpec((CHUNK, D_MODEL), lambda b, c: (b * nc + c, 0)),
                   pl.BlockSpec((CHUNK, D_MODEL), lambda b, c: (b * nc + nc - 1 - c, 0))],
        out_shape=[out, out],
        scratch_shapes=[pltpu.VMEM((2 * C_HEADS, 2 * LANES, CHUNK), F32), pltpu.VMEM((2 * C_HEADS, LANES), F32)],
        compiler_params=_cparams(("parallel", "arbitrary")),
        name="mlstm",
    )(q, k, kt, vt, g, q, k, kt, vt, g)


MOE_BM = 512
DISPATCH_TM = 512


def _dispatch_kernel(zero_ref, dest_ref, x_ref, xs_hbm, zbuf, sem):
    tm = x_ref.shape[0]

    @pl.when(pl.program_id(0) == 0)
    def _():
        zbuf[...] = jnp.zeros_like(zbuf)
        for z in range(2 * N_EXPERTS):
            cp = pltpu.make_async_copy(zbuf, xs_hbm.at[pl.ds(pl.multiple_of(zero_ref[z], MOE_BM), MOE_BM), :], sem)
            cp.start()
            cp.wait()

    def issue(t, c):
        for kk in range(TOP_K):
            d = dest_ref[0, kk, t]
            pltpu.make_async_copy(x_ref.at[pl.ds(t, 1), :], xs_hbm.at[pl.ds(d, 1), :], sem).start()
        return c

    lax.fori_loop(0, tm, issue, 0, unroll=8)
    for _ in range(TOP_K):
        pltpu.make_async_copy(x_ref, xs_hbm.at[pl.ds(0, tm), :], sem).wait()


def _dispatch(x1, dest, zero_rows, n_slots):
    t = x1.shape[0]
    tm = DISPATCH_TM
    return pl.pallas_call(
        _dispatch_kernel,
        grid_spec=pltpu.PrefetchScalarGridSpec(
            num_scalar_prefetch=1,
            grid=(t // tm,),
            in_specs=[pl.BlockSpec((1, TOP_K, tm), lambda i, z: (i, 0, 0), memory_space=pltpu.SMEM),
                      pl.BlockSpec((tm, D_MODEL), lambda i, z: (i, 0))],
            out_specs=pl.BlockSpec(memory_space=pl.ANY),
            scratch_shapes=[pltpu.VMEM((MOE_BM, D_MODEL), F32), pltpu.SemaphoreType.DMA(())],
        ),
        out_shape=jax.ShapeDtypeStruct((n_slots, D_MODEL), F32),
        compiler_params=_cparams(("arbitrary",)),
        name="moe_dispatch",
    )(zero_rows, dest, x1)


def _experts_kernel(be_ref, nv_ref, xs_ref, wg_ref, wu_ref, wd_ref, ys_ref):
    i = pl.program_id(0)
    nv = nv_ref[i]

    @pl.when(nv > 0)
    def _():
        xb = xs_ref[...].astype(BF16)
        hg = jnp.dot(xb, wg_ref[...], preferred_element_type=F32)
        hu = jnp.dot(xb, wu_ref[...], preferred_element_type=F32)
        hh = (jax.nn.silu(hg) * hu).astype(BF16)
        ys_ref[...] = jnp.dot(hh, wd_ref[...], preferred_element_type=F32)

    @pl.when(nv == 0)
    def _():
        ys_ref[...] = jnp.zeros_like(ys_ref)


def _experts(xs, block_e, n_valid, wg, wu, wd):
    n_slots = xs.shape[0]
    bm = MOE_BM
    wspec = lambda s: pl.BlockSpec((None,) + s, lambda i, be, nv: (be[i], 0, 0))
    return pl.pallas_call(
        _experts_kernel,
        grid_spec=pltpu.PrefetchScalarGridSpec(
            num_scalar_prefetch=2,
            grid=(n_slots // bm,),
            in_specs=[pl.BlockSpec((bm, D_MODEL), lambda i, be, nv: (i, 0)),
                      wspec((D_MODEL, D_EXPERT)), wspec((D_MODEL, D_EXPERT)), wspec((D_EXPERT, D_MODEL))],
            out_specs=pl.BlockSpec((bm, D_MODEL), lambda i, be, nv: (i, 0)),
        ),
        out_shape=jax.ShapeDtypeStruct((n_slots, D_MODEL), F32),
        compiler_params=_cparams(("arbitrary",)),
        name="moe_experts",
    )(block_e, n_valid, xs, wg, wu, wd)


def _combine_kernel(dest_ref, next_ref, ys_hbm, x_ref, gate_ref, lg_ref, lb_ref, o_ref, ybuf, sems):
    tm = x_ref.shape[0]
    i = pl.program_id(0)
    n = pl.num_programs(0)
    slot = i % 2

    def gather(idx_ref, s):
        def issue(t, c):
            for kk in range(TOP_K):
                d = idx_ref[0, kk, t]
                pltpu.make_async_copy(ys_hbm.at[pl.ds(d, 1), :], ybuf.at[s, kk, pl.ds(t, 1), :], sems.at[s]).start()
            return c

        lax.fori_loop(0, tm, issue, 0, unroll=8)

    @pl.when(i == 0)
    def _():
        gather(dest_ref, 0)

    @pl.when(i + 1 < n)
    def _():
        gather(next_ref, 1 - slot)

    for kk in range(TOP_K):
        pltpu.make_async_copy(ys_hbm.at[pl.ds(0, tm), :], ybuf.at[slot, kk], sems.at[slot]).wait()
    y = gate_ref[:, 0:1] * ybuf[slot, 0] + gate_ref[:, 1:2] * ybuf[slot, 1]
    o_ref[...] = _layer_norm(ALPHA * x_ref[...] + y, lg_ref[...], lb_ref[...])


def _combine(ys, dest, x1, gate, lg, lb, tile0, n_tiles):
    tm = DISPATCH_TM
    last = tile0 + n_tiles - 1
    return pl.pallas_call(
        _combine_kernel,
        grid=(n_tiles,),
        in_specs=[pl.BlockSpec((1, TOP_K, tm), lambda i: (tile0 + i, 0, 0), memory_space=pltpu.SMEM),
                  pl.BlockSpec((1, TOP_K, tm), lambda i: (jnp.minimum(tile0 + i + 1, last), 0, 0),
                               memory_space=pltpu.SMEM),
                  pl.BlockSpec(memory_space=pl.ANY),
                  pl.BlockSpec((tm, D_MODEL), lambda i: (tile0 + i, 0)),
                  pl.BlockSpec((tm, TOP_K), lambda i: (tile0 + i, 0)),
                  _full(lg.shape), _full(lb.shape)],
        out_specs=pl.BlockSpec((tm, D_MODEL), lambda i: (i, 0)),
        out_shape=jax.ShapeDtypeStruct((n_tiles * tm, D_MODEL), F32),
        scratch_shapes=[pltpu.VMEM((2, TOP_K, tm, D_MODEL), F32), pltpu.SemaphoreType.DMA((2,))],
        compiler_params=_cparams(("arbitrary",)),
        name="moe_combine",
    )(dest, dest, ys, x1, gate, lg, lb)


def _slots(ri, cnt):
    t = ri.shape[1]
    bm = MOE_BM
    counts = cnt[:, 0].astype(jnp.int32)
    padded = (counts + bm - 1) // bm * bm
    pad_end = jnp.cumsum(padded)
    pad_start = pad_end - padded
    experts, rank = ri[0:TOP_K], ri[TOP_K:2 * TOP_K]
    start = jnp.sum(jnp.where(experts[:, :, None] == jnp.arange(N_EXPERTS)[None, None, :], pad_start[None, None, :], 0),
                    axis=-1)
    dest = (start + rank).astype(jnp.int32)
    n_blocks = (t * TOP_K) // bm + N_EXPERTS
    blk0 = jnp.arange(n_blocks, dtype=jnp.int32) * bm
    ge = blk0[:, None] >= pad_end[None, :]
    block_e = jnp.minimum(jnp.sum(ge, axis=1), N_EXPERTS - 1).astype(jnp.int32)
    seg_end = jnp.sum(jnp.where(block_e[:, None] == jnp.arange(N_EXPERTS)[None, :], (pad_start + counts)[None, :], 0),
                      axis=1)
    n_valid = jnp.where(blk0 < pad_end[-1], jnp.clip(seg_end - blk0, 0, bm), 0).astype(jnp.int32)
    tm = DISPATCH_TM
    dest = dest.reshape(TOP_K, t // tm, tm).transpose(1, 0, 2)
    tail = jnp.minimum(pad_end[-1] + jnp.arange(N_EXPERTS, dtype=jnp.int32) * bm, (n_blocks - 1) * bm)
    zero_rows = jnp.concatenate([jnp.maximum(pad_end - bm, 0), tail]).astype(jnp.int32)
    return dest, block_e, n_valid, zero_rows, n_blocks * bm


def _moe(x1, ri, rf, cnt, wg, wu, wd, lg, lb, splits):
    dest, block_e, n_valid, zero_rows, n_slots = _slots(ri, cnt)
    xs = _dispatch(x1, dest, zero_rows, n_slots)
    ys = _experts(xs, block_e, n_valid, wg, wu, wd)
    gate = rf[0:TOP_K].T
    return [_combine(ys, dest, x1, gate, lg, lb, t0, nt) for t0, nt in splits]


def _rope_table(seq):
    inv = jnp.power(ROPE_THETA, -jnp.arange(0, QK_ROPE, 2, dtype=F32) / QK_ROPE)
    ang = jnp.arange(seq, dtype=F32)[:, None] * inv[None, :]
    cos, sin = jnp.cos(ang), jnp.sin(ang)
    one = jnp.ones((seq, QK_NOPE), F32)
    z64 = jnp.zeros((seq, QK_NOPE), F32)
    z32 = jnp.zeros((seq, LANES - QK_NOPE - QK_ROPE), F32)
    return jnp.concatenate([one, cos, cos, z32, z64, sin, sin, z32], axis=1)


def _swap_pairs(w):
    half = QK_ROPE // 2
    return jnp.concatenate([-w[..., half:], w[..., :half]], axis=-1)


def _even_weights(w_in, gm_g, gm_b, w_sp, b_sp, q_g, w_uq, kv_g, w_ukv):
    o0, o1, o2, o3 = GM_WIDTH, 2 * GM_WIDTH, 2 * GM_WIDTH + Q_LORA, 2 * GM_WIDTH + Q_LORA + KV_LORA
    w_kr = w_in[:, o3:]
    z64 = jnp.zeros((D_MODEL, QK_NOPE), F32)
    z32 = jnp.zeros((D_MODEL, LANES - QK_NOPE - QK_ROPE), F32)
    wa = jnp.concatenate([w_in[:, :o3], z64, w_kr, z32, z64, _swap_pairs(w_kr), z32], axis=1).astype(BF16)
    wq = w_uq.reshape(Q_LORA, MLA_HEADS, QK_NOPE + QK_ROPE)
    nope, rope = wq[..., :QK_NOPE], wq[..., QK_NOPE:]
    zq32 = jnp.zeros((Q_LORA, MLA_HEADS, LANES - QK_NOPE - QK_ROPE), F32)
    zq64 = jnp.zeros((Q_LORA, MLA_HEADS, QK_NOPE), F32)
    q1 = jnp.concatenate([nope, rope, zq32], axis=-1).reshape(Q_LORA, D_MODEL)
    q2 = jnp.concatenate([zq64, _swap_pairs(rope), zq32], axis=-1).reshape(Q_LORA, D_MODEL)
    wq = jnp.concatenate([q1, q2], axis=1).astype(BF16)
    wkv = w_ukv.reshape(KV_LORA, MLA_HEADS, QK_NOPE + V_HEAD)
    kn = jnp.concatenate([wkv[..., :QK_NOPE], jnp.zeros((KV_LORA, MLA_HEADS, LANES - QK_NOPE), F32)], axis=-1)
    wkv = jnp.concatenate([kn.reshape(KV_LORA, D_MODEL), wkv[..., QK_NOPE:].reshape(KV_LORA, MLA_HEADS * V_HEAD)],
                          axis=1).astype(BF16)
    bsp = jnp.broadcast_to(b_sp[:, :, None], (GM_HEADS, CHUNK, LANES)).astype(F32)
    return (wa, gm_g.reshape(1, GM_WIDTH), gm_b.reshape(1, GM_WIDTH), w_sp.astype(BF16), bsp,
            q_g.reshape(1, Q_LORA), wq, kv_g.reshape(1, KV_LORA), wkv)


def _router_weights(router_w):
    whi = router_w.astype(BF16)
    wlo = (router_w - whi.astype(F32)).astype(BF16)
    pad = jnp.zeros((D_MODEL, LANES - 2 * N_EXPERTS), BF16)
    top = jnp.concatenate([whi, wlo, pad], axis=1)
    bot = jnp.concatenate([whi, jnp.zeros_like(wlo), pad], axis=1)
    return jnp.concatenate([top, bot], axis=0)


def kernel(x_prompt, x_sample, even_w_in, gm_norm_g, gm_norm_b, gm_w_spatial, gm_b_spatial, mla_q_norm_g, mla_w_uq, mla_kv_norm_g, mla_w_ukv, even_w_out, mlstm_w_in, mlstm_b_gates, mlstm_norm_g, mlstm_norm_b, mlstm_w_out, router_w, router_bias, moe_w_gate, moe_w_up, moe_w_down, ln_mix_g, ln_mix_b, ln_ffn_g, ln_ffn_b):
    bp, sp, _ = x_prompt.shape
    bs, ss, _ = x_sample.shape
    n_prompt = bp * sp
    x = jnp.concatenate([x_prompt.reshape(n_prompt, D_MODEL), x_sample.reshape(bs * ss, D_MODEL)], axis=0)
    groups = ((0, bp, sp), (n_prompt, bs, ss))
    tab = _rope_table(max(sp, ss))
    row = lambda a: a.reshape(1, -1).astype(F32)
    strict_upper = (jnp.arange(OUT_TM)[:, None] < jnp.arange(OUT_TM)[None, :]).astype(BF16)
    route_consts = (_router_weights(router_w), jnp.broadcast_to(router_bias.astype(F32)[:, None], (N_EXPERTS, LANES)),
                    strict_upper)

    for l in range(DEPTH):
        j = l // 2
        if l % 2 == 0:
            wts = _even_weights(even_w_in[j], gm_norm_g[j], gm_norm_b[j], gm_w_spatial[j], gm_b_spatial[j],
                                mla_q_norm_g[j], mla_w_uq[j], mla_kv_norm_g[j], mla_w_ukv[j])
            a, q, k, vt = _even_in(x, tab, wts, n_prompt, sp, ss)
            o_p, o_s = [_attention(q, k, vt, r0, b, s) for r0, b, s in groups]
            consts = (even_w_out[j].astype(BF16), row(ln_mix_g[l]), row(ln_mix_b[l])) + route_consts
            acts = ((a, None), (o_p, 0), (o_s, 1))
            x1, ri, rf, cnt = _mixer_out(_even_out_kernel, acts, x, consts, n_prompt)
        else:
            w_in = mlstm_w_in[j]
            wg = jnp.concatenate([w_in[:, 4 * D_MODEL:], jnp.zeros((D_MODEL, LANES - 4 * C_HEADS), F32)], axis=1)
            bg = jnp.concatenate([mlstm_b_gates[j], jnp.zeros((LANES - 4 * C_HEADS,), F32)]).reshape(1, LANES)
            q, k, kt, vt, og, g = _odd_in(x, w_in[:, :4 * D_MODEL].astype(BF16), wg.astype(BF16), bg)
            (hf_p, hb_p), (hf_s, hb_s) = [_mlstm(q, k, kt, vt, g, r0, b, s) for r0, b, s in groups]
            consts = (row(mlstm_norm_g[j]), row(mlstm_norm_b[j]), mlstm_w_out[j].astype(BF16),
                      row(ln_mix_g[l]), row(ln_mix_b[l])) + route_consts
            acts = ((hf_p, 0), (hb_p, 0), (hf_s, 1), (hb_s, 1), (og, None))
            x1, ri, rf, cnt = _mixer_out(_odd_out_kernel, acts, x, consts, n_prompt)
        p_tiles, all_tiles = n_prompt // DISPATCH_TM, x.shape[0] // DISPATCH_TM
        splits = ((0, all_tiles),) if l + 1 < DEPTH else ((0, p_tiles), (p_tiles, all_tiles - p_tiles))
        outs = _moe(x1, ri, rf, cnt, moe_w_gate[l].astype(BF16), moe_w_up[l].astype(BF16),
                    moe_w_down[l].astype(BF16), row(ln_ffn_g[l]), row(ln_ffn_b[l]), splits)
        x = outs[0]

    y_prompt, y_sample = outs
    return y_prompt.reshape(bp, sp, D_MODEL), y_sample.reshape(bs, ss, D_MODEL)
```

```python
import functools

import jax
import jax.numpy as jnp
from jax import lax
from jax.experimental import pallas as pl
from jax.experimental.pallas import tpu as pltpu

F32 = jnp.float32
BF16 = jnp.bfloat16

D_MODEL = 1024
DEPTH = 4
CHUNK = 128
GM_WIDTH = 512
GM_HEADS = 4
MLA_HEADS = 8
QK_NOPE = 64
QK_ROPE = 32
V_HEAD = 64
Q_LORA = 384
KV_LORA = 256
ROPE_THETA = 10000.0
C_HEADS = 8
C_HEAD_DIM = 128
N_EXPERTS = 16
N_GROUPS = 4
EXPERTS_PER_GROUP = 4
TOP_K = 2
D_EXPERT = 512
ALPHA = (2 * DEPTH) ** 0.25
EPS = 1e-5
LOG2_E = 1.4426950408889634
LANES = 128

VMEM_LIMIT = 56 * 1024 * 1024


def _cparams(sem):
    return pltpu.CompilerParams(dimension_semantics=sem, vmem_limit_bytes=VMEM_LIMIT)


def _layer_norm(y, g, b):
    mu = jnp.mean(y, axis=-1, keepdims=True)
    yc = y - mu
    var = jnp.mean(yc * yc, axis=-1, keepdims=True)
    return yc * lax.rsqrt(var + EPS) * g + b


def _rms_norm(y, g):
    return y * lax.rsqrt(jnp.mean(y * y, axis=-1, keepdims=True) + EPS) * g


def _full(shape):
    return pl.BlockSpec(shape, lambda *_: (0,) * len(shape))


EVEN_TM = 512


def _even_in_kernel(x_ref, tab_ref, wa_ref, gmg_ref, gmb_ref, wsp_ref, bsp_ref, qg_ref, wq_ref, kvg_ref, wkv_ref,
                    a_ref, qt_ref, k_ref, vt_ref):
    tm = x_ref.shape[0]
    z = jnp.dot(x_ref[...].astype(BF16), wa_ref[...], preferred_element_type=F32)
    ct = tab_ref[:, 0:LANES]
    st = tab_ref[:, LANES:2 * LANES]

    for g in range(GM_HEADS):
        lo = GM_WIDTH + g * LANES
        gv = jax.nn.gelu(z[:, lo:lo + LANES], approximate=True)
        gv = _layer_norm(gv, gmg_ref[:, g * LANES:(g + 1) * LANES], gmb_ref[:, g * LANES:(g + 1) * LANES]).astype(BF16)
        gu = jax.nn.gelu(z[:, g * LANES:(g + 1) * LANES], approximate=True)
        for c in range(tm // CHUNK):
            rows = slice(c * CHUNK, (c + 1) * CHUNK)
            sv = jnp.dot(wsp_ref[g], gv[rows], preferred_element_type=F32) + bsp_ref[g]
            a_ref[rows, g * LANES:(g + 1) * LANES] = (gu[rows] * sv).astype(a_ref.dtype)

    cq = _rms_norm(z[:, 2 * GM_WIDTH:2 * GM_WIDTH + Q_LORA], qg_ref[...]).astype(BF16)
    qq = jnp.dot(cq, wq_ref[...], preferred_element_type=F32)
    scale = (QK_NOPE + QK_ROPE) ** -0.5 * LOG2_E
    for h in range(MLA_HEADS):
        q1 = qq[:, h * LANES:(h + 1) * LANES]
        q2 = qq[:, D_MODEL + h * LANES:D_MODEL + (h + 1) * LANES]
        qt_ref[h * LANES:(h + 1) * LANES, :] = ((q1 * ct + q2 * st) * scale).T.astype(qt_ref.dtype)

    base = 2 * GM_WIDTH + Q_LORA
    ckv = _rms_norm(z[:, base:base + KV_LORA], kvg_ref[...]).astype(BF16)
    kv = jnp.dot(ckv, wkv_ref[...], preferred_element_type=F32)
    base += KV_LORA
    kr = z[:, base:base + LANES] * ct + z[:, base + LANES:base + 2 * LANES] * st
    for h in range(MLA_HEADS):
        k_ref[:, h * LANES:(h + 1) * LANES] = (kv[:, h * LANES:(h + 1) * LANES] + kr).astype(k_ref.dtype)
    vt_ref[...] = kv[:, D_MODEL:].T.astype(vt_ref.dtype)


def _even_in(x, tab, wts, n_prompt_rows, s_prompt, s_sample):
    t = x.shape[0]
    tm = EVEN_TM
    wa, gmg, gmb, wsp, bsp, qg, wq, kvg, wkv = wts

    def tab_map(i):
        return (jnp.where(i < n_prompt_rows // tm, i % (s_prompt // tm), i % (s_sample // tm)), 0)

    row = lambda w: pl.BlockSpec((tm, w), lambda i: (i, 0))
    return pl.pallas_call(
        _even_in_kernel,
        grid=(t // tm,),
        in_specs=[row(D_MODEL), pl.BlockSpec((tm, 2 * LANES), tab_map), _full(wa.shape), _full(gmg.shape),
                  _full(gmb.shape), _full(wsp.shape), _full(bsp.shape), _full(qg.shape), _full(wq.shape),
                  _full(kvg.shape), _full(wkv.shape)],
        out_specs=[row(GM_WIDTH), pl.BlockSpec((D_MODEL, tm), lambda i: (0, i)), row(D_MODEL),
                   pl.BlockSpec((MLA_HEADS * V_HEAD, tm), lambda i: (0, i))],
        out_shape=[jax.ShapeDtypeStruct((t, GM_WIDTH), BF16), jax.ShapeDtypeStruct((D_MODEL, t), BF16),
                   jax.ShapeDtypeStruct((t, D_MODEL), BF16), jax.ShapeDtypeStruct((MLA_HEADS * V_HEAD, t), BF16)],
        compiler_params=_cparams(("parallel",)),
        name="even_in",
    )(x, tab, wa, gmg, gmb, wsp, bsp, qg, wq, kvg, wkv)


ATT_TQ = 512
ATT_TK = 256
ATT_SUM_ROWS = 16
ATT_ROWS = 64
ATT_S_SLOTS = 2
ATT_P_SLOTS = 1
ATT_UNROLL = 8


def _attn_kernel(qt_ref, k_ref, vt_ref, o_ref, *scratch, seq, tk):
    tq = qt_ref.shape[1]
    n = seq // tk
    qs = [qt_ref[j * LANES:(j + 1) * LANES, :] for j in range(2)]
    slots = [scratch[2 * i:2 * i + 2] for i in range(ATT_S_SLOTS)]
    p_slots = [scratch[2 * (ATT_S_SLOTS + i):2 * (ATT_S_SLOTS + i) + 2] for i in range(ATT_P_SLOTS)]

    def scores(j, c):
        r0 = pl.multiple_of(jnp.minimum(c, n - 1) * tk, tk)
        kc = k_ref[pl.ds(r0, tk), j * LANES:(j + 1) * LANES]
        return jnp.dot(kc, qs[j], preferred_element_type=F32)

    def consume(c, cur, nxt, probs, carry):
        for j in range(2):
            nxt[j][...] = scores(j, c + 1)
        r0 = pl.multiple_of(c * tk, tk)
        new = []
        for j in range(2):
            m, acc = carry[j]
            m_new = m
            for r in range(tk // ATT_ROWS):
                blk = cur[j][r * ATT_ROWS:(r + 1) * ATT_ROWS, :]
                m_new = jnp.maximum(m_new, jnp.max(blk, axis=0, keepdims=True))
            for r in range(tk // ATT_ROWS):
                blk = cur[j][r * ATT_ROWS:(r + 1) * ATT_ROWS, :]
                probs[j][r * ATT_ROWS:(r + 1) * ATT_ROWS, :] = jnp.exp2((blk - m_new).astype(BF16))
            a = jnp.exp2(m - m_new)
            vtc = jnp.concatenate([vt_ref[j * V_HEAD:(j + 1) * V_HEAD, pl.ds(r0, tk)], ones_rows], axis=0)
            acc = a * acc + jnp.dot(vtc, probs[j][...], preferred_element_type=F32)
            new.append((m_new, acc))
        return tuple(new)

    def step(i, carry):
        for u in range(ATT_UNROLL):
            carry = consume(ATT_UNROLL * i + u, slots[u % ATT_S_SLOTS], slots[(u + 1) % ATT_S_SLOTS],
                            p_slots[u % ATT_P_SLOTS], carry)
        return carry

    ones_rows = jnp.ones((ATT_SUM_ROWS, tk), BF16)
    for j in range(2):
        slots[0][j][...] = scores(j, 0)
    init = tuple((jnp.full((1, tq), -jnp.inf, F32), jnp.zeros((V_HEAD + ATT_SUM_ROWS, tq), F32)) for _ in range(2))
    res = lax.fori_loop(0, n // ATT_UNROLL, step, init)
    ot = jnp.concatenate([acc[0:V_HEAD] / acc[V_HEAD:V_HEAD + 1] for _, acc in res], axis=0)
    o_ref[...] = ot.T.astype(o_ref.dtype)


def _attention(qt, k, vt, row0, batch, seq):
    tq = ATT_TQ
    nq = seq // tq
    assert seq % (ATT_TK * ATT_UNROLL) == 0 and seq % tq == 0 and row0 % seq == 0
    return pl.pallas_call(
        functools.partial(_attn_kernel, seq=seq, tk=ATT_TK),
        grid=(batch, MLA_HEADS // 2, nq),
        in_specs=[pl.BlockSpec((2 * LANES, tq), lambda b, h, i: (h, row0 // tq + b * nq + i)),
                  pl.BlockSpec((seq, 2 * LANES), lambda b, h, i: (row0 // seq + b, h)),
                  pl.BlockSpec((2 * V_HEAD, seq), lambda b, h, i: (h, row0 // seq + b))],
        out_specs=pl.BlockSpec((tq, LANES), lambda b, h, i: (b * nq + i, h)),
        out_shape=jax.ShapeDtypeStruct((batch * seq, MLA_HEADS * V_HEAD), BF16),
        scratch_shapes=([pltpu.VMEM((ATT_TK, tq), F32)] * (2 * ATT_S_SLOTS)
                        + [pltpu.VMEM((ATT_TK, tq), BF16)] * (2 * ATT_P_SLOTS)),
        compiler_params=_cparams(("parallel", "parallel", "parallel")),
        name="attention",
    )(qt, k, vt)


OUT_TM = 512
OUT_SPLIT = 2
ROUTE_ROWS = 8


def _first_of(vals, target):
    idx = jnp.full(target.shape, len(vals) - 1, jnp.int32)
    for j in range(len(vals) - 2, -1, -1):
        idx = jnp.where(vals[j] == target, j, idx)
    return idx


def _pick(vals, idx):
    out = vals[-1]
    for j in range(len(vals) - 2, -1, -1):
        out = jnp.where(idx == j, vals[j], out)
    return out


def _route_rows(logits, bias_col):
    scores = jax.nn.sigmoid(logits)
    biased = scores + bias_col
    b = [biased[e:e + 1, :] for e in range(N_EXPERTS)]
    s = [scores[e:e + 1, :] for e in range(N_EXPERTS)]
    gs = []
    for g in range(N_GROUPS):
        v0, v1, v2, v3 = b[EXPERTS_PER_GROUP * g:EXPERTS_PER_GROUP * (g + 1)]
        hi1, lo1 = jnp.maximum(v0, v1), jnp.minimum(v0, v1)
        hi2, lo2 = jnp.maximum(v2, v3), jnp.minimum(v2, v3)
        gs.append(jnp.maximum(hi1, hi2) + jnp.maximum(jnp.minimum(hi1, hi2), jnp.maximum(lo1, lo2)))
    g_sel = _first_of(gs, functools.reduce(jnp.maximum, gs))
    v = [_pick([b[EXPERTS_PER_GROUP * g + j] for g in range(N_GROUPS)], g_sel) for j in range(EXPERTS_PER_GROUP)]
    sc = [_pick([s[EXPERTS_PER_GROUP * g + j] for g in range(N_GROUPS)], g_sel) for j in range(EXPERTS_PER_GROUP)]
    i1 = _first_of(v, functools.reduce(jnp.maximum, v))
    rest = [jnp.where(i1 == j, -jnp.inf, v[j]) for j in range(EXPERTS_PER_GROUP)]
    i2 = _first_of(rest, functools.reduce(jnp.maximum, rest))
    s1, s2 = _pick(sc, i1), _pick(sc, i2)
    den = s1 + s2
    return g_sel * EXPERTS_PER_GROUP + i1, g_sel * EXPERTS_PER_GROUP + i2, s1 / den, s2 / den


def _post_mix(mix_of, x_ref, lg_ref, lb_ref, wr_ref, rb_ref, su_ref, x1_ref, ri_ref, rf_ref, cnt_ref, carry_scr):
    @pl.when(pl.program_id(0) == 0)
    def _():
        carry_scr[...] = jnp.zeros_like(carry_scr)

    tm = x_ref.shape[0]
    lgts = []
    for r in range(OUT_SPLIT):
        rows = slice(r * tm // OUT_SPLIT, (r + 1) * tm // OUT_SPLIT)
        x1 = _layer_norm(ALPHA * x_ref[rows, :] + mix_of(rows), lg_ref[...], lb_ref[...])
        x1_ref[rows, :] = x1
        hi = x1.astype(BF16)
        lo = (x1 - hi.astype(F32)).astype(BF16)
        lg = jnp.dot(hi, wr_ref[0:D_MODEL, :], preferred_element_type=F32)
        lg = lg + jnp.dot(lo, wr_ref[D_MODEL:, :], preferred_element_type=F32)
        lgts.append(lg.T)
    lgt = jnp.concatenate(lgts, axis=1)
    e1, e2, g1, g2 = _route_rows(lgt[0:N_EXPERTS] + lgt[N_EXPERTS:2 * N_EXPERTS], rb_ref[:, 0:1])

    erow = lax.broadcasted_iota(jnp.int32, (N_EXPERTS, tm), 0)
    o1, o2 = erow == e1, erow == e2
    oh = jnp.where(o1 | o2, 1.0, 0.0)
    before = jnp.dot(oh.astype(BF16), su_ref[...], preferred_element_type=F32) + carry_scr[:, 0:1]
    r1 = jnp.sum(jnp.where(o1, before, 0.0), axis=0, keepdims=True).astype(jnp.int32)
    r2 = jnp.sum(jnp.where(o2, before, 0.0), axis=0, keepdims=True).astype(jnp.int32)
    carry_scr[...] = carry_scr[...] + jnp.sum(oh, axis=1, keepdims=True)
    cnt_ref[...] = carry_scr[...]
    ri_ref[...] = jnp.concatenate([e1, e2, r1, r2, jnp.zeros((ROUTE_ROWS - 4, tm), jnp.int32)], axis=0)
    rf_ref[...] = jnp.concatenate([g1, g2, jnp.zeros((ROUTE_ROWS - 2, tm), F32)], axis=0)


def _even_out_kernel(a_ref, op_ref, os_ref, x_ref, w_ref, *rest, prompt_tiles):
    is_prompt = pl.program_id(0) < prompt_tiles

    def mix_of(rows):
        o = jnp.where(is_prompt, op_ref[rows, :], os_ref[rows, :])
        mix = jnp.dot(a_ref[rows, :], w_ref[0:GM_WIDTH, :], preferred_element_type=F32)
        return mix + jnp.dot(o, w_ref[GM_WIDTH:, :], preferred_element_type=F32)

    _post_mix(mix_of, x_ref, *rest)


def _odd_out_kernel(hfp_ref, hbp_ref, hfs_ref, hbs_ref, og_ref, x_ref, ng_ref, nb_ref, w_ref, *rest, prompt_tiles):
    is_prompt = pl.program_id(0) < prompt_tiles

    def mix_of(rows):
        parts = []
        for h in range(C_HEADS):
            cols = slice(h * LANES, (h + 1) * LANES)
            hf = jnp.where(is_prompt, hfp_ref[rows, cols], hfs_ref[rows, cols])
            hb = jnp.where(is_prompt, hbp_ref[rows, cols], hbs_ref[rows, cols])
            hs = _layer_norm(hf.astype(F32) + hb.astype(F32), ng_ref[:, cols], nb_ref[:, cols])
            parts.append((hs * og_ref[rows, cols].astype(F32)).astype(BF16))
        return jnp.dot(jnp.concatenate(parts, axis=1), w_ref[...], preferred_element_type=F32)

    _post_mix(mix_of, x_ref, *rest)


def _mixer_out(body, acts, x, consts, n_prompt_rows):
    t = x.shape[0]
    tm = OUT_TM
    pt = n_prompt_rows // tm
    maps = {None: lambda i: (i, 0), 0: lambda i: (jnp.minimum(i, pt - 1), 0), 1: lambda i: (jnp.maximum(i - pt, 0), 0)}
    row = lambda w: pl.BlockSpec((tm, w), lambda i: (i, 0))
    route_spec = pl.BlockSpec((ROUTE_ROWS, tm), lambda i: (0, i))
    return pl.pallas_call(
        functools.partial(body, prompt_tiles=pt),
        grid=(t // tm,),
        in_specs=[pl.BlockSpec((tm, a.shape[1]), maps[g]) for a, g in acts] + [row(D_MODEL)]
        + [_full(c.shape) for c in consts],
        out_specs=[row(D_MODEL), route_spec, route_spec, _full((N_EXPERTS, LANES))],
        out_shape=[jax.ShapeDtypeStruct((t, D_MODEL), F32), jax.ShapeDtypeStruct((ROUTE_ROWS, t), jnp.int32),
                   jax.ShapeDtypeStruct((ROUTE_ROWS, t), F32), jax.ShapeDtypeStruct((N_EXPERTS, LANES), F32)],
        scratch_shapes=[pltpu.VMEM((N_EXPERTS, LANES), F32)],
        compiler_params=_cparams(("arbitrary",)),
        name=body.__name__.strip("_"),
    )(*[a for a, _ in acts], x, *consts)


ODD_TM = 512


def _odd_in_kernel(x_ref, w_ref, wg_ref, bg_ref, q_ref, k_ref, kt_ref, vt_ref, og_ref, g_ref):
    xb = x_ref[...].astype(BF16)
    q_ref[...] = jnp.dot(xb, w_ref[:, 0:D_MODEL], preferred_element_type=F32).astype(q_ref.dtype)
    k = jnp.dot(xb, w_ref[:, D_MODEL:2 * D_MODEL], preferred_element_type=F32) * (C_HEAD_DIM ** -0.5)
    k_ref[...] = k.astype(k_ref.dtype)
    kt_ref[...] = k.T.astype(kt_ref.dtype)
    v = jnp.dot(xb, w_ref[:, 2 * D_MODEL:3 * D_MODEL], preferred_element_type=F32)
    vt_ref[...] = v.T.astype(vt_ref.dtype)
    o = jnp.dot(xb, w_ref[:, 3 * D_MODEL:4 * D_MODEL], preferred_element_type=F32)
    og_ref[...] = jax.nn.sigmoid(o).astype(og_ref.dtype)
    g = jnp.dot(xb, wg_ref[...], preferred_element_type=F32) + bg_ref[...]
    lane = lax.broadcasted_iota(jnp.int32, g.shape, 1)
    is_f = ((lane >= C_HEADS) & (lane < 2 * C_HEADS)) | ((lane >= 3 * C_HEADS) & (lane < 4 * C_HEADS))
    g_ref[...] = jnp.where(is_f, jax.nn.log_sigmoid(g), g)


def _odd_in(x, w, wg, bg):
    t = x.shape[0]
    tm = ODD_TM
    row = lambda w_: pl.BlockSpec((tm, w_), lambda i: (i, 0))
    col = pl.BlockSpec((D_MODEL, tm), lambda i: (0, i))
    big = jax.ShapeDtypeStruct((t, D_MODEL), BF16)
    big_t = jax.ShapeDtypeStruct((D_MODEL, t), BF16)
    return pl.pallas_call(
        _odd_in_kernel,
        grid=(t // tm,),
        in_specs=[row(D_MODEL), _full(w.shape), _full(wg.shape), _full(bg.shape)],
        out_specs=[row(D_MODEL), row(D_MODEL), col, col, row(D_MODEL), row(LANES)],
        out_shape=[big, big, big_t, big_t, big, jax.ShapeDtypeStruct((t, LANES), F32)],
        compiler_params=_cparams(("parallel",)),
        name="odd_in",
    )(x, w, wg, bg)


def _mlstm_kernel(qf_ref, kf_ref, ktf_ref, vtf_ref, gf_ref, qb_ref, kb_ref, ktb_ref, vtb_ref, gb_ref,
                  hf_ref, hb_ref, c_scr, m_scr):
    L = CHUNK

    @pl.when(pl.program_id(1) == 0)
    def _():
        c_scr[...] = jnp.zeros_like(c_scr)
        m_scr[...] = jnp.zeros_like(m_scr)

    row_i = lax.broadcasted_iota(jnp.int32, (L, L), 0)
    col_i = lax.broadcasted_iota(jnp.int32, (L, L), 1)
    ones_row = jnp.where(row_i == 0, 1.0, 0.0).astype(BF16)
    nt = (((1,), (1,)), ((), ()))
    refs = ((qf_ref, kf_ref, ktf_ref, vtf_ref, gf_ref, hf_ref), (qb_ref, kb_ref, ktb_ref, vtb_ref, gb_ref, hb_ref))
    units = [(d, h) for d in range(2) for h in range(C_HEADS)]
    cols = lambda h: slice(h * LANES, (h + 1) * LANES)

    gate = []
    for d in range(2):
        past = (col_i <= row_i) if d == 0 else (col_i >= row_i)
        vis = (row_i <= col_i) if d == 0 else (row_i >= col_i)
        gates = refs[d][4][...]
        csum = jnp.dot(past.astype(F32), gates, preferred_element_type=F32, precision=lax.Precision.HIGHEST)
        gate.append((vis, gates, csum, gates.T, csum.T))

    qk_t, cq, c_old, m_old = {}, {}, {}, {}
    for u in units:
        d, h = u
        q = refs[d][0][:, cols(h)]
        sd = 2 * h + d
        c_old[u] = c_scr[sd]
        m_old[u] = m_scr[sd:sd + 1, :]
        qk_t[u] = lax.dot_general(refs[d][1][:, cols(h)], q, nt, preferred_element_type=F32)
        cq[u] = lax.dot_general(c_old[u].astype(BF16), q, nt, preferred_element_type=F32)

    att_t, kw_t, w_row, m_row, decay, m_new = {}, {}, {}, {}, {}, {}
    for u in units:
        d, h = u
        vis, gates, csum, gates_t, csum_t = gate[d]
        i_col, f_col = 2 * C_HEADS * d + h, 2 * C_HEADS * d + C_HEADS + h
        last = L - 1 if d == 0 else 0
        r_col = gates[:, i_col:i_col + 1] - csum[:, f_col:f_col + 1]
        b_row = csum_t[f_col:f_col + 1, :]
        ic_row = gates_t[i_col:i_col + 1, :]
        dmat_t = jnp.where(vis, r_col + b_row, -jnp.inf)
        inter = b_row + m_old[u]
        m_row[u] = jnp.maximum(inter, jnp.max(dmat_t, axis=0, keepdims=True))
        w_row[u] = jnp.exp(inter - m_row[u])
        att_t[u] = (jnp.exp(dmat_t - m_row[u]) * qk_t[u]).astype(BF16)
        b_last = jnp.broadcast_to(b_row[:, last:last + 1], (1, L))
        g_row = b_last - b_row + ic_row
        m_new[u] = jnp.maximum(b_last + m_old[u], jnp.max(g_row, axis=1, keepdims=True))
        decay[u] = jnp.exp(b_last + m_old[u] - m_new[u])
        kw_t[u] = (refs[d][2][cols(h), :].astype(F32) * jnp.exp(g_row - m_new[u])).astype(BF16)

    pv_t, upd = {}, {}
    for u in units:
        d, h = u
        vt_ext = jnp.concatenate([refs[d][3][cols(h), :], ones_row], axis=0)
        pv_t[u] = jnp.dot(vt_ext, att_t[u], preferred_element_type=F32)
        upd[u] = lax.dot_general(vt_ext, kw_t[u], nt, preferred_element_type=F32)

    for u in units:
        d, h = u
        sd = 2 * h + d
        num_t = w_row[u] * cq[u] + pv_t[u]
        den = num_t[LANES:LANES + 1, :]
        h_t = num_t[0:LANES, :] / jnp.maximum(jnp.abs(den), jnp.exp(-m_row[u]))
        h_ref = refs[d][5]
        h_ref[:, cols(h)] = h_t.T.astype(h_ref.dtype)
        c_scr[sd] = decay[u] * c_old[u] + upd[u]
        m_scr[sd:sd + 1, :] = m_new[u]


def _mlstm(q, k, kt, vt, g, row0, batch, seq):
    nc = seq // CHUNK
    c0 = row0 // CHUNK
    fwd = lambda b, c: c0 + b * nc + c
    bwd = lambda b, c: c0 + b * nc + nc - 1 - c
    big = lambda m: pl.BlockSpec((CHUNK, D_MODEL), lambda b, c: (m(b, c), 0))
    big_t = lambda m: pl.BlockSpec((D_MODEL, CHUNK), lambda b, c: (0, m(b, c)))
    small = lambda m: pl.BlockSpec((CHUNK, LANES), lambda b, c: (m(b, c), 0))
    out = jax.ShapeDtypeStruct((batch * seq, D_MODEL), BF16)
    return pl.pallas_call(
        _mlstm_kernel,
        grid=(batch, nc),
        in_specs=[big(fwd), big(fwd), big_t(fwd), big_t(fwd), small(fwd),
                  big(bwd), big(bwd), big_t(bwd), big_t(bwd), small(bwd)],
        out_specs=[pl.BlockSpec((CHUNK, D_MODEL), lambda b, c: (b * nc + c, 0)),
                   pl.BlockSpec((CHUNK, D_MODEL), lambda b, c: (b * nc + nc - 1 - c, 0))],
        out_shape=[out, out],
        scratch_shapes=[pltpu.VMEM((2 * C_HEADS, 2 * LANES, CHUNK), F32), pltpu.VMEM((2 * C_HEADS, LANES), F32)],
        compiler_params=_cparams(("parallel", "arbitrary")),
        name="mlstm",
    )(q, k, kt, vt, g, q, k, kt, vt, g)


MOE_BM = 512
DISPATCH_TM = 512


def _dispatch_kernel(zero_ref, dest_ref, x_ref, xs_hbm, zbuf, sem):
    tm = x_ref.shape[0]

    @pl.when(pl.program_id(0) == 0)
    def _():
        zbuf[...] = jnp.zeros_like(zbuf)
        for z in range(2 * N_EXPERTS):
            cp = pltpu.make_async_copy(zbuf, xs_hbm.at[pl.ds(pl.multiple_of(zero_ref[z], MOE_BM), MOE_BM), :], sem)
            cp.start()
            cp.wait()

    def issue(t, c):
        for kk in range(TOP_K):
            d = dest_ref[0, kk, t]
            pltpu.make_async_copy(x_ref.at[pl.ds(t, 1), :], xs_hbm.at[pl.ds(d, 1), :], sem).start()
        return c

    lax.fori_loop(0, tm, issue, 0, unroll=8)
    for _ in range(TOP_K):
        pltpu.make_async_copy(x_ref, xs_hbm.at[pl.ds(0, tm), :], sem).wait()


def _dispatch(x1, dest, zero_rows, n_slots):
    t = x1.shape[0]
    tm = DISPATCH_TM
    return pl.pallas_call(
        _dispatch_kernel,
        grid_spec=pltpu.PrefetchScalarGridSpec(
            num_scalar_prefetch=1,
            grid=(t // tm,),
            in_specs=[pl.BlockSpec((1, TOP_K, tm), lambda i, z: (i, 0, 0), memory_space=pltpu.SMEM),
                      pl.BlockSpec((tm, D_MODEL), lambda i, z: (i, 0))],
            out_specs=pl.BlockSpec(memory_space=pl.ANY),
            scratch_shapes=[pltpu.VMEM((MOE_BM, D_MODEL), F32), pltpu.SemaphoreType.DMA(())],
        ),
        out_shape=jax.ShapeDtypeStruct((n_slots, D_MODEL), F32),
        compiler_params=_cparams(("arbitrary",)),
        name="moe_dispatch",
    )(zero_rows, dest, x1)


def _experts_kernel(be_ref, nv_ref, xs_ref, wg_ref, wu_ref, wd_ref, ys_ref):
    i = pl.program_id(0)
    nv = nv_ref[i]

    @pl.when(nv > 0)
    def _():
        xb = xs_ref[...].astype(BF16)
        hg = jnp.dot(xb, wg_ref[...], preferred_element_type=F32)
        hu = jnp.dot(xb, wu_ref[...], preferred_element_type=F32)
        hh = (jax.nn.silu(hg) * hu).astype(BF16)
        ys_ref[...] = jnp.dot(hh, wd_ref[...], preferred_element_type=F32)

    @pl.when(nv == 0)
    def _():
        ys_ref[...] = jnp.zeros_like(ys_ref)


def _experts(xs, block_e, n_valid, wg, wu, wd):
    n_slots = xs.shape[0]
    bm = MOE_BM
    wspec = lambda s: pl.BlockSpec((None,) + s, lambda i, be, nv: (be[i], 0, 0))
    return pl.pallas_call(
        _experts_kernel,
        grid_spec=pltpu.PrefetchScalarGridSpec(
            num_scalar_prefetch=2,
            grid=(n_slots // bm,),
            in_specs=[pl.BlockSpec((bm, D_MODEL), lambda i, be, nv: (i, 0)),
                      wspec((D_MODEL, D_EXPERT)), wspec((D_MODEL, D_EXPERT)), wspec((D_EXPERT, D_MODEL))],
            out_specs=pl.BlockSpec((bm, D_MODEL), lambda i, be, nv: (i, 0)),
        ),
        out_shape=jax.ShapeDtypeStruct((n_slots, D_MODEL), F32),
        compiler_params=_cparams(("arbitrary",)),
        name="moe_experts",
    )(block_e, n_valid, xs, wg, wu, wd)


def _combine_kernel(dest_ref, next_ref, ys_hbm, x_ref, gate_ref, lg_ref, lb_ref, o_ref, ybuf, sems):
    tm = x_ref.shape[0]
    i = pl.program_id(0)
    n = pl.num_programs(0)
    slot = i % 2

    def gather(idx_ref, s):
        def issue(t, c):
            for kk in range(TOP_K):
                d = idx_ref[0, kk, t]
                pltpu.make_async_copy(ys_hbm.at[pl.ds(d, 1), :], ybuf.at[s, kk, pl.ds(t, 1), :], sems.at[s]).start()
            return c

        lax.fori_loop(0, tm, issue, 0, unroll=8)

    @pl.when(i == 0)
    def _():
        gather(dest_ref, 0)

    @pl.when(i + 1 < n)
    def _():
        gather(next_ref, 1 - slot)

    for kk in range(TOP_K):
        pltpu.make_async_copy(ys_hbm.at[pl.ds(0, tm), :], ybuf.at[slot, kk], sems.at[slot]).wait()
    y = gate_ref[:, 0:1] * ybuf[slot, 0] + gate_ref[:, 1:2] * ybuf[slot, 1]
    o_ref[...] = _layer_norm(ALPHA * x_ref[...] + y, lg_ref[...], lb_ref[...])


def _combine(ys, dest, x1, gate, lg, lb, tile0, n_tiles):
    tm = DISPATCH_TM
    last = tile0 + n_tiles - 1
    return pl.pallas_call(
        _combine_kernel,
        grid=(n_tiles,),
        in_specs=[pl.BlockSpec((1, TOP_K, tm), lambda i: (tile0 + i, 0, 0), memory_space=pltpu.SMEM),
                  pl.BlockSpec((1, TOP_K, tm), lambda i: (jnp.minimum(tile0 + i + 1, last), 0, 0),
                               memory_space=pltpu.SMEM),
                  pl.BlockSpec(memory_space=pl.ANY),
                  pl.BlockSpec((tm, D_MODEL), lambda i: (tile0 + i, 0)),
                  pl.BlockSpec((tm, TOP_K), lambda i: (tile0 + i, 0)),
                  _full(lg.shape), _full(lb.shape)],
        out_specs=pl.BlockSpec((tm, D_MODEL), lambda i: (i, 0)),
        out_shape=jax.ShapeDtypeStruct((n_tiles * tm, D_MODEL), F32),
        scratch_shapes=[pltpu.VMEM((2, TOP_K, tm, D_MODEL), F32), pltpu.SemaphoreType.DMA((2,))],
        compiler_params=_cparams(("arbitrary",)),
        name="moe_combine",
    )(dest, dest, ys, x1, gate, lg, lb)


def _slots(ri, cnt):
    t = ri.shape[1]
    bm = MOE_BM
    counts = cnt[:, 0].astype(jnp.int32)
    padded = (counts + bm - 1) // bm * bm
    pad_end = jnp.cumsum(padded)
    pad_start = pad_end - padded
    experts, rank = ri[0:TOP_K], ri[TOP_K:2 * TOP_K]
    start = jnp.sum(jnp.where(experts[:, :, None] == jnp.arange(N_EXPERTS)[None, None, :], pad_start[None, None, :], 0),
                    axis=-1)
    dest = (start + rank).astype(jnp.int32)
    n_blocks = (t * TOP_K) // bm + N_EXPERTS
    blk0 = jnp.arange(n_blocks, dtype=jnp.int32) * bm
    ge = blk0[:, None] >= pad_end[None, :]
    block_e = jnp.minimum(jnp.sum(ge, axis=1), N_EXPERTS - 1).astype(jnp.int32)
    seg_end = jnp.sum(jnp.where(block_e[:, None] == jnp.arange(N_EXPERTS)[None, :], (pad_start + counts)[None, :], 0),
                      axis=1)
    n_valid = jnp.where(blk0 < pad_end[-1], jnp.clip(seg_end - blk0, 0, bm), 0).astype(jnp.int32)
    tm = DISPATCH_TM
    dest = dest.reshape(TOP_K, t // tm, tm).transpose(1, 0, 2)
    tail = jnp.minimum(pad_end[-1] + jnp.arange(N_EXPERTS, dtype=jnp.int32) * bm, (n_blocks - 1) * bm)
    zero_rows = jnp.concatenate([jnp.maximum(pad_end - bm, 0), tail]).astype(jnp.int32)
    return dest, block_e, n_valid, zero_rows, n_blocks * bm


def _moe(x1, ri, rf, cnt, wg, wu, wd, lg, lb, splits):
    dest, block_e, n_valid, zero_rows, n_slots = _slots(ri, cnt)
    xs = _dispatch(x1, dest, zero_rows, n_slots)
    ys = _experts(xs, block_e, n_valid, wg, wu, wd)
    gate = rf[0:TOP_K].T
    return [_combine(ys, dest, x1, gate, lg, lb, t0, nt) for t0, nt in splits]


def _rope_table(seq):
    inv = jnp.power(ROPE_THETA, -jnp.arange(0, QK_ROPE, 2, dtype=F32) / QK_ROPE)
    ang = jnp.arange(seq, dtype=F32)[:, None] * inv[None, :]
    cos, sin = jnp.cos(ang), jnp.sin(ang)
    one = jnp.ones((seq, QK_NOPE), F32)
    z64 = jnp.zeros((seq, QK_NOPE), F32)
    z32 = jnp.zeros((seq, LANES - QK_NOPE - QK_ROPE), F32)
    return jnp.concatenate([one, cos, cos, z32, z64, sin, sin, z32], axis=1)


def _swap_pairs(w):
    half = QK_ROPE // 2
    return jnp.concatenate([-w[..., half:], w[..., :half]], axis=-1)


def _even_weights(w_in, gm_g, gm_b, w_sp, b_sp, q_g, w_uq, kv_g, w_ukv):
    o0, o1, o2, o3 = GM_WIDTH, 2 * GM_WIDTH, 2 * GM_WIDTH + Q_LORA, 2 * GM_WIDTH + Q_LORA + KV_LORA
    w_kr = w_in[:, o3:]
    z64 = jnp.zeros((D_MODEL, QK_NOPE), F32)
    z32 = jnp.zeros((D_MODEL, LANES - QK_NOPE - QK_ROPE), F32)
    wa = jnp.concatenate([w_in[:, :o3], z64, w_kr, z32, z64, _swap_pairs(w_kr), z32], axis=1).astype(BF16)
    wq = w_uq.reshape(Q_LORA, MLA_HEADS, QK_NOPE + QK_ROPE)
    nope, rope = wq[..., :QK_NOPE], wq[..., QK_NOPE:]
    zq32 = jnp.zeros((Q_LORA, MLA_HEADS, LANES - QK_NOPE - QK_ROPE), F32)
    zq64 = jnp.zeros((Q_LORA, MLA_HEADS, QK_NOPE), F32)
    q1 = jnp.concatenate([nope, rope, zq32], axis=-1).reshape(Q_LORA, D_MODEL)
    q2 = jnp.concatenate([zq64, _swap_pairs(rope), zq32], axis=-1).reshape(Q_LORA, D_MODEL)
    wq = jnp.concatenate([q1, q2], axis=1).astype(BF16)
    wkv = w_ukv.reshape(KV_LORA, MLA_HEADS, QK_NOPE + V_HEAD)
    kn = jnp.concatenate([wkv[..., :QK_NOPE], jnp.zeros((KV_LORA, MLA_HEADS, LANES - QK_NOPE), F32)], axis=-1)
    wkv = jnp.concatenate([kn.reshape(KV_LORA, D_MODEL), wkv[..., QK_NOPE:].reshape(KV_LORA, MLA_HEADS * V_HEAD)],
                          axis=1).astype(BF16)
    bsp = jnp.broadcast_to(b_sp[:, :, None], (GM_HEADS, CHUNK, LANES)).astype(F32)
    return (wa, gm_g.reshape(1, GM_WIDTH), gm_b.reshape(1, GM_WIDTH), w_sp.astype(BF16), bsp,
            q_g.reshape(1, Q_LORA), wq, kv_g.reshape(1, KV_LORA), wkv)


def _router_weights(router_w):
    whi = router_w.astype(BF16)
    wlo = (router_w - whi.astype(F32)).astype(BF16)
    pad = jnp.zeros((D_MODEL, LANES - 2 * N_EXPERTS), BF16)
    top = jnp.concatenate([whi, wlo, pad], axis=1)
    bot = jnp.concatenate([whi, jnp.zeros_like(wlo), pad], axis=1)
    return jnp.concatenate([top, bot], axis=0)


def kernel(x_prompt, x_sample, even_w_in, gm_norm_g, gm_norm_b, gm_w_spatial, gm_b_spatial, mla_q_norm_g, mla_w_uq, mla_kv_norm_g, mla_w_ukv, even_w_out, mlstm_w_in, mlstm_b_gates, mlstm_norm_g, mlstm_norm_b, mlstm_w_out, router_w, router_bias, moe_w_gate, moe_w_up, moe_w_down, ln_mix_g, ln_mix_b, ln_ffn_g, ln_ffn_b):
    bp, sp, _ = x_prompt.shape
    bs, ss, _ = x_sample.shape
    n_prompt = bp * sp
    x = jnp.concatenate([x_prompt.reshape(n_prompt, D_MODEL), x_sample.reshape(bs * ss, D_MODEL)], axis=0)
    groups = ((0, bp, sp), (n_prompt, bs, ss))
    tab = _rope_table(max(sp, ss))
    row = lambda a: a.reshape(1, -1).astype(F32)
    strict_upper = (jnp.arange(OUT_TM)[:, None] < jnp.arange(OUT_TM)[None, :]).astype(BF16)
    route_consts = (_router_weights(router_w), jnp.broadcast_to(router_bias.astype(F32)[:, None], (N_EXPERTS, LANES)),
                    strict_upper)

    for l in range(DEPTH):
        j = l // 2
        if l % 2 == 0:
            wts = _even_weights(even_w_in[j], gm_norm_g[j], gm_norm_b[j], gm_w_spatial[j], gm_b_spatial[j],
                                mla_q_norm_g[j], mla_w_uq[j], mla_kv_norm_g[j], mla_w_ukv[j])
            a, qt, k, vt = _even_in(x, tab, wts, n_prompt, sp, ss)
            o_p, o_s = [_attention(qt, k, vt, r0, b, s) for r0, b, s in groups]
            consts = (even_w_out[j].astype(BF16), row(ln_mix_g[l]), row(ln_mix_b[l])) + route_consts
            acts = ((a, None), (o_p, 0), (o_s, 1))
            x1, ri, rf, cnt = _mixer_out(_even_out_kernel, acts, x, consts, n_prompt)
        else:
            w_in = mlstm_w_in[j]
            wg = jnp.concatenate([w_in[:, 4 * D_MODEL:], jnp.zeros((D_MODEL, LANES - 4 * C_HEADS), F32)], axis=1)
            bg = jnp.concatenate([mlstm_b_gates[j], jnp.zeros((LANES - 4 * C_HEADS,), F32)]).reshape(1, LANES)
            q, k, kt, vt, og, g = _odd_in(x, w_in[:, :4 * D_MODEL].astype(BF16), wg.astype(BF16), bg)
            (hf_p, hb_p), (hf_s, hb_s) = [_mlstm(q, k, kt, vt, g, r0, b, s) for r0, b, s in groups]
            consts = (row(mlstm_norm_g[j]), row(mlstm_norm_b[j]), mlstm_w_out[j].astype(BF16),
                      row(ln_mix_g[l]), row(ln_mix_b[l])) + route_consts
            acts = ((hf_p, 0), (hb_p, 0), (hf_s, 1), (hb_s, 1), (og, None))
            x1, ri, rf, cnt = _mixer_out(_odd_out_kernel, acts, x, consts, n_prompt)
        p_tiles, all_tiles = n_prompt // DISPATCH_TM, x.shape[0] // DISPATCH_TM
        splits = ((0, all_tiles),) if l + 1 < DEPTH else ((0, p_tiles), (p_tiles, all_tiles - p_tiles))
        outs = _moe(x1, ri, rf, cnt, moe_w_gate[l].astype(BF16), moe_w_up[l].astype(BF16),
                    moe_w_down[l].astype(BF16), row(ln_ffn_g[l]), row(ln_ffn_b[l]), splits)
        x = outs[0]

    y_prompt, y_sample = outs
    return y_prompt.reshape(bp, sp, D_MODEL), y_sample.reshape(bs, ss, D_MODEL)
```
